```python
import jax, jax.numpy as jnp
from jax import lax
import numpy as np

D_MODEL = 2048
BATCH = 4
SEQ = 4096
DEPTH = 1

CHUNK = 64

FOX_HEADS = 8
FOX_HEAD_DIM = 128
FOX_WIDTH = FOX_HEADS * FOX_HEAD_DIM
Q_BLOCK = 128
FORGET_BIAS_INIT = 3.0

POOL_WINDOWS = (2, 4, 8, 16)
POOL_GROUPS = len(POOL_WINDOWS)
POOL_WIDTH = 1024
POOL_GROUP_DIM = POOL_WIDTH // POOL_GROUPS

N_BRANCHES = 2
IN_SPLITS = (
    FOX_WIDTH,
    2 * FOX_WIDTH,
    3 * FOX_WIDTH,
    3 * FOX_WIDTH + FOX_HEADS,
    3 * FOX_WIDTH + FOX_HEADS + POOL_WIDTH,
)
IN_COLS = 3 * FOX_WIDTH + FOX_HEADS + POOL_WIDTH + N_BRANCHES * D_MODEL

PEER_HEADS = 8
PEER_KEYS = 128
PEER_EXPERTS = PEER_KEYS * PEER_KEYS
PEER_QUERY_DIM = 256
PEER_HALF = PEER_QUERY_DIM // 2
PEER_TOPK = 16
PEER_TOKEN_BLOCK = 128

RMS_EPS = 1e-6

kernel_name = "hybrid_fox_pool_peer_block"


def rms_norm(x, g):
    xf = x.astype(jnp.float32)
    y = xf * lax.rsqrt(jnp.mean(xf * xf, axis=-1, keepdims=True) + RMS_EPS)
    return (y * g.astype(jnp.float32)).astype(x.dtype)


def forgetting_attention(q, k, v, log_f):
    seq = q.shape[2]
    c = jnp.cumsum(log_f, axis=-1)
    scale = FOX_HEAD_DIM ** -0.5
    outs = []
    for blk in range(seq // Q_BLOCK):
        q0 = blk * Q_BLOCK
        q1 = q0 + Q_BLOCK
        qb = q[:, :, q0:q1]
        kb = k[:, :, :q1]
        vb = v[:, :, :q1]
        logits = jnp.einsum('bhqd,bhkd->bhqk', qb, kb,
                            preferred_element_type=jnp.float32) * scale
        logits = logits + (c[:, :, q0:q1, None] - c[:, :, None, :q1])
        causal = jnp.arange(q0, q1)[:, None] >= jnp.arange(q1)[None, :]
        logits = jnp.where(causal, logits, -jnp.inf)
        p = jax.nn.softmax(logits, axis=-1)
        outs.append(jnp.einsum('bhqk,bhkd->bhqd', p.astype(v.dtype), vb))
    return jnp.concatenate(outs, axis=2)


def multiscale_pool(p, w_groups, scale):
    b, s, _ = p.shape
    pg = p.reshape(b, s, POOL_GROUPS, POOL_GROUP_DIM).astype(jnp.float32)
    csum = jnp.cumsum(pg, axis=1)
    pos = jnp.arange(s)
    outs = []
    for gi, w in enumerate(POOL_WINDOWS):
        cs = csum[:, :, gi]
        lagged = jnp.pad(cs, ((0, 0), (w, 0), (0, 0)))[:, :s]
        count = jnp.minimum(pos + 1, w).astype(jnp.float32)[None, :, None]
        outs.append((cs - lagged) / count - pg[:, :, gi])
    mixed = jnp.stack(outs, axis=2).astype(p.dtype)
    mixed = jnp.einsum('bsgc,gcd->bsgd', mixed, w_groups)
    return mixed.reshape(b, s, POOL_WIDTH) * scale


def peer(xn, w_query, sub_keys_1, sub_keys_2, expert_u, expert_v):
    b, s, d = xn.shape
    t = b * s
    xt = xn.reshape(t, d)
    q = (xt @ w_query).reshape(t, PEER_HEADS, PEER_QUERY_DIM).astype(jnp.float32)
    q1, q2 = q[..., :PEER_HALF], q[..., PEER_HALF:]
    s1 = jnp.einsum('thd,hkd->thk', q1, sub_keys_1.astype(jnp.float32))
    s2 = jnp.einsum('thd,hkd->thk', q2, sub_keys_2.astype(jnp.float32))
    v1, i1 = lax.top_k(s1, PEER_TOPK)
    v2, i2 = lax.top_k(s2, PEER_TOPK)
    cand = (v1[..., :, None] + v2[..., None, :]).reshape(t, PEER_HEADS, PEER_TOPK * PEER_TOPK)
    cand_idx = (i1[..., :, None] * PEER_KEYS + i2[..., None, :]).reshape(
        t, PEER_HEADS, PEER_TOPK * PEER_TOPK)
    top_s, top_pos = lax.top_k(cand, PEER_TOPK)
    expert_idx = jnp.take_along_axis(cand_idx, top_pos, axis=-1)
    gates = jax.nn.softmax(top_s, axis=-1).astype(xn.dtype)

    def token_block(args):
        xb, idx, g = args
        u = jnp.take(expert_u, idx, axis=0)
        a = jnp.einsum('thkd,td->thk', u, xb)
        hid = jax.nn.gelu(a) * g
        vsel = jnp.take(expert_v, idx, axis=0)
        return jnp.einsum('thk,thkd->td', hid, vsel)

    nb = t // PEER_TOKEN_BLOCK
    out = lax.map(token_block, (
        xt.reshape(nb, PEER_TOKEN_BLOCK, d),
        expert_idx.reshape(nb, PEER_TOKEN_BLOCK, PEER_HEADS, PEER_TOPK),
        gates.reshape(nb, PEER_TOKEN_BLOCK, PEER_HEADS, PEER_TOPK),
    ))
    return out.reshape(b, s, d)


def setup_inputs(seed: int = 0) -> dict:
    key = jax.random.key(seed)
    ks = jax.random.split(key, 20)
    f32 = jnp.float32

    def nrm(k, shape, std):
        return jax.random.normal(k, shape, f32) * std

    return {
        "x": nrm(ks[0], (BATCH, SEQ, D_MODEL), 1.0),
        "norm1_g": 1.0 + nrm(ks[1], (DEPTH, D_MODEL), 0.02),
        "w_in": nrm(ks[2], (DEPTH, D_MODEL, IN_COLS), D_MODEL ** -0.5),
        "forget_bias": FORGET_BIAS_INIT + nrm(ks[3], (DEPTH, FOX_HEADS), 0.5),
        "q_norm_g": 1.0 + nrm(ks[4], (DEPTH, FOX_HEAD_DIM), 0.02),
        "k_norm_g": 1.0 + nrm(ks[5], (DEPTH, FOX_HEAD_DIM), 0.02),
        "pool_group_w": nrm(ks[6], (DEPTH, POOL_GROUPS, POOL_GROUP_DIM, POOL_GROUP_DIM),
                            POOL_GROUP_DIM ** -0.5),
        "pool_scale": 1.0 + nrm(ks[7], (DEPTH, POOL_WIDTH), 0.02),
        "w_branch_attn": nrm(ks[8], (DEPTH, FOX_WIDTH, D_MODEL), FOX_WIDTH ** -0.5),
        "w_branch_pool": nrm(ks[9], (DEPTH, POOL_WIDTH, D_MODEL), POOL_WIDTH ** -0.5),
        "w_out": nrm(ks[10], (DEPTH, D_MODEL, D_MODEL), D_MODEL ** -0.5),
        "norm2_g": 1.0 + nrm(ks[11], (DEPTH, D_MODEL), 0.02),
        "peer_w_query": nrm(ks[12], (DEPTH, D_MODEL, PEER_HEADS * PEER_QUERY_DIM), D_MODEL ** -0.5),
        "peer_sub_keys_1": nrm(ks[13], (DEPTH, PEER_HEADS, PEER_KEYS, PEER_HALF), PEER_HALF ** -0.5),
        "peer_sub_keys_2": nrm(ks[14], (DEPTH, PEER_HEADS, PEER_KEYS, PEER_HALF), PEER_HALF ** -0.5),
        "peer_expert_u": nrm(ks[15], (DEPTH, PEER_EXPERTS, D_MODEL), D_MODEL ** -0.5),
        "peer_expert_v": nrm(ks[16], (DEPTH, PEER_EXPERTS, D_MODEL), PEER_HEADS ** -0.5),
    }


def reference(x, norm1_g, w_in, forget_bias, q_norm_g, k_norm_g, pool_group_w, pool_scale,
              w_branch_attn, w_branch_pool, w_out, norm2_g, peer_w_query, peer_sub_keys_1,
              peer_sub_keys_2, peer_expert_u, peer_expert_v):
    b, s, _ = x.shape
    for l in range(DEPTH):
        h = rms_norm(x, norm1_g[l])
        proj = h @ w_in[l]
        q, k, v, fl, pin, gl = jnp.split(proj, IN_SPLITS, axis=-1)

        q = rms_norm(q.reshape(b, s, FOX_HEADS, FOX_HEAD_DIM), q_norm_g[l]).transpose(0, 2, 1, 3)
        k = rms_norm(k.reshape(b, s, FOX_HEADS, FOX_HEAD_DIM), k_norm_g[l]).transpose(0, 2, 1, 3)
        v = v.reshape(b, s, FOX_HEADS, FOX_HEAD_DIM).transpose(0, 2, 1, 3)
        log_f = jax.nn.log_sigmoid(fl.astype(jnp.float32)
                                   + forget_bias[l].astype(jnp.float32)).transpose(0, 2, 1)
        attn = forgetting_attention(q, k, v, log_f)
        y_attn = attn.transpose(0, 2, 1, 3).reshape(b, s, FOX_WIDTH) @ w_branch_attn[l]

        y_pool = multiscale_pool(pin, pool_group_w[l], pool_scale[l]) @ w_branch_pool[l]

        gates = jax.nn.sigmoid(gl.astype(jnp.float32)).astype(x.dtype).reshape(
            b, s, N_BRANCHES, D_MODEL)
        merged = gates[:, :, 0] * y_attn + gates[:, :, 1] * y_pool
        x = x + merged @ w_out[l]

        x = x + peer(rms_norm(x, norm2_g[l]), peer_w_query[l], peer_sub_keys_1[l],
                     peer_sub_keys_2[l], peer_expert_u[l], peer_expert_v[l])
    return x
```

```python
import functools

import jax
import jax.numpy as jnp
from jax import lax
from jax.experimental import pallas as pl
from jax.experimental.pallas import tpu as pltpu

F32 = jnp.float32
BF16 = jnp.bfloat16

RMS_EPS = 1e-6
FOX_HEADS = 8
HEAD_DIM = 128
FOX_WIDTH = FOX_HEADS * HEAD_DIM
POOL_WINDOWS = (2, 4, 8, 16)
POOL_WIDTH = 1024
POOL_GROUP_DIM = POOL_WIDTH // len(POOL_WINDOWS)
POOL_HALO = 16
PEER_HEADS = 8
PEER_KEYS = 128
PEER_HALF = 128
PEER_TOPK = 16
PEER_PAIRS = PEER_HEADS * PEER_TOPK

LANES = 128
VMEM_LIMIT_BYTES = 56 * 1024 * 1024


def _params(*sem):
    return pltpu.CompilerParams(dimension_semantics=sem, vmem_limit_bytes=VMEM_LIMIT_BYTES)


def _rms(x, g):
    ms = jnp.mean(x * x, axis=-1, keepdims=True)
    return x * lax.rsqrt(ms + RMS_EPS) * g


def _norm_proj_kernel(x_ref, g_ref, w_ref, gain_ref, o_ref, h_ref, *, mode, n_first, tn):
    j = pl.program_id(1)

    @pl.when(j == 0)
    def _():
        h_ref[...] = _rms(x_ref[...], g_ref[...]).astype(BF16)

    y = jnp.dot(h_ref[...], w_ref[...], preferred_element_type=F32)

    if mode == "qkv":
        @pl.when(j < n_first)
        def _():
            for hh in range(tn // HEAD_DIM):
                sl = slice(hh * HEAD_DIM, (hh + 1) * HEAD_DIM)
                blk = y[:, sl]
                ms = jnp.mean(blk * blk, axis=-1, keepdims=True)
                o_ref[:, sl] = (blk * lax.rsqrt(ms + RMS_EPS) * gain_ref[:, sl]).astype(o_ref.dtype)

        @pl.when(j >= n_first)
        def _():
            o_ref[...] = y.astype(o_ref.dtype)
    else:
        @pl.when(j < n_first)
        def _():
            o_ref[...] = y.astype(o_ref.dtype)

        @pl.when(j >= n_first)
        def _():
            o_ref[...] = (1.0 / (1.0 + jnp.exp(-y))).astype(o_ref.dtype)


def _norm_proj(x2, g, w, gain, *, mode, n_first, out_dtype, tm, tn):
    t, d = x2.shape
    n = w.shape[1]
    kern = functools.partial(_norm_proj_kernel, mode=mode, n_first=n_first, tn=tn)
    return pl.pallas_call(
        kern,
        grid=(t // tm, n // tn),
        in_specs=[
            pl.BlockSpec((tm, d), lambda i, j: (i, 0)),
            pl.BlockSpec((1, d), lambda i, j: (0, 0)),
            pl.BlockSpec((d, tn), lambda i, j: (0, j)),
            pl.BlockSpec((1, tn), lambda i, j: (0, j)),
        ],
        out_specs=pl.BlockSpec((tm, tn), lambda i, j: (i, j)),
        out_shape=jax.ShapeDtypeStruct((t, n), out_dtype),
        scratch_shapes=[pltpu.VMEM((tm, d), BF16)],
        compiler_params=_params("parallel", "arbitrary"),
        name="norm_proj_" + mode,
    )(x2, g, w, gain)


def _logf_kernel(x_ref, g_ref, w_ref, b_ref, ct_ref, carry_ref, *, tm):
    s = pl.program_id(1)

    @pl.when(s == 0)
    def _():
        carry_ref[...] = jnp.zeros_like(carry_ref)

    h = _rms(x_ref[...], g_ref[...]).astype(BF16)
    fl = jnp.dot(h, w_ref[...], preferred_element_type=F32) + b_ref[...]
    lf = jnp.minimum(fl, 0.0) - jnp.log(1.0 + jnp.exp(-jnp.abs(fl)))
    r = lax.broadcasted_iota(jnp.int32, (tm, tm), 0)
    c = lax.broadcasted_iota(jnp.int32, (tm, tm), 1)
    tri = (r >= c).astype(BF16)
    p1 = lf.astype(BF16)
    r1 = lf - p1.astype(F32)
    p2 = r1.astype(BF16)
    p3 = (r1 - p2.astype(F32)).astype(BF16)
    cs = (jnp.dot(tri, p1, preferred_element_type=F32)
          + jnp.dot(tri, p2, preferred_element_type=F32)
          + jnp.dot(tri, p3, preferred_element_type=F32)) + carry_ref[...]
    carry_ref[...] = cs[tm - 1:tm, :]
    ct_ref[...] = cs.T[:FOX_HEADS, :]


def _logf_cumsum(x2, g, w_fl, b_fl, *, batch, seq, tm):
    t, d = x2.shape
    ns = seq // tm
    return pl.pallas_call(
        functools.partial(_logf_kernel, tm=tm),
        grid=(batch, ns),
        in_specs=[
            pl.BlockSpec((tm, d), lambda b, s: (b * ns + s, 0)),
            pl.BlockSpec((1, d), lambda b, s: (0, 0)),
            pl.BlockSpec((d, LANES), lambda b, s: (0, 0)),
            pl.BlockSpec((1, LANES), lambda b, s: (0, 0)),
        ],
        out_specs=pl.BlockSpec((FOX_HEADS, tm), lambda b, s: (0, b * ns + s)),
        out_shape=jax.ShapeDtypeStruct((FOX_HEADS, t), F32),
        scratch_shapes=[pltpu.VMEM((1, LANES), F32)],
        compiler_params=_params("parallel", "arbitrary"),
        name="logf_cumsum",
    )(x2, g, w_fl, b_fl)


def _attn_kernel(q_ref, k_ref, v_ref, ck_ref, o_ref, m_ref, l_ref, acc_ref, *, tq, tk, nk):
    qi = pl.program_id(2)
    kj = pl.program_id(3)

    @pl.when(kj == 0)
    def _():
        m_ref[...] = jnp.full_like(m_ref, -jnp.inf)
        l_ref[...] = jnp.zeros_like(l_ref)
        acc_ref[...] = jnp.zeros_like(acc_ref)

    @pl.when(kj * tk <= qi * tq + (tq - 1))
    def _():
        s = lax.dot_general(q_ref[...], k_ref[...], (((1,), (1,)), ((), ())),
                            preferred_element_type=F32)
        s = s - ck_ref[...]
        rows = qi * tq + lax.broadcasted_iota(jnp.int32, (tq, tk), 0)
        cols = kj * tk + lax.broadcasted_iota(jnp.int32, (tq, tk), 1)
        s = jnp.where(rows >= cols, s, -jnp.inf)
        m_prev = m_ref[...]
        m_new = jnp.maximum(m_prev, jnp.max(s, axis=-1, keepdims=True))
        alpha = jnp.exp(m_prev - m_new)
        p = jnp.exp(s - m_new)
        l_ref[...] = alpha * l_ref[...] + jnp.sum(p, axis=-1, keepdims=True)
        acc_ref[...] = alpha * acc_ref[...] + jnp.dot(p.astype(BF16), v_ref[...],
                                                      preferred_element_type=F32)
        m_ref[...] = m_new

    @pl.when(kj == nk - 1)
    def _():
        o_ref[...] = (acc_ref[...] / l_ref[...]).astype(o_ref.dtype)


def _attention(qkv, ct3, *, batch, seq, tq, tk):
    t = qkv.shape[0]
    nq, nk = seq // tq, seq // tk

    def last_kv(qi):
        return (qi * tq + (tq - 1)) // tk

    return pl.pallas_call(
        functools.partial(_attn_kernel, tq=tq, tk=tk, nk=nk),
        grid=(batch, FOX_HEADS, nq, nk),
        in_specs=[
            pl.BlockSpec((tq, HEAD_DIM), lambda b, h, qi, kj: (b * nq + qi, h)),
            pl.BlockSpec((tk, HEAD_DIM),
                         lambda b, h, qi, kj: (b * nk + jnp.minimum(kj, last_kv(qi)), FOX_HEADS + h)),
            pl.BlockSpec((tk, HEAD_DIM),
                         lambda b, h, qi, kj: (b * nk + jnp.minimum(kj, last_kv(qi)), 2 * FOX_HEADS + h)),
            pl.BlockSpec((None, 1, tk),
                         lambda b, h, qi, kj: (h, 0, b * nk + jnp.minimum(kj, last_kv(qi)))),
        ],
        out_specs=pl.BlockSpec((tq, HEAD_DIM), lambda b, h, qi, kj: (b * nq + qi, h)),
        out_shape=jax.ShapeDtypeStruct((t, FOX_WIDTH), BF16),
        scratch_shapes=[pltpu.VMEM((tq, 1), F32), pltpu.VMEM((tq, 1), F32),
                        pltpu.VMEM((tq, HEAD_DIM), F32)],
        compiler_params=_params("parallel", "parallel", "parallel", "arbitrary"),
        name="fox_attention",
    )(qkv, qkv, qkv, ct3)


def _pool_kernel(p_ref, w_ref, sc_ref, o_ref, ext_ref, *, tm):
    s = pl.program_id(1)

    @pl.when(s == 0)
    def _():
        ext_ref[0:POOL_HALO, :] = jnp.zeros((POOL_HALO, POOL_WIDTH), F32)

    @pl.when(s > 0)
    def _():
        ext_ref[0:POOL_HALO, :] = ext_ref[tm:tm + POOL_HALO, :]

    ext_ref[POOL_HALO:POOL_HALO + tm, :] = p_ref[...]
    pos = s * tm + lax.broadcasted_iota(jnp.int32, (tm, 1), 0)
    for gi, w in enumerate(POOL_WINDOWS):
        cols = slice(gi * POOL_GROUP_DIM, (gi + 1) * POOL_GROUP_DIM)
        cur = ext_ref[POOL_HALO:POOL_HALO + tm, cols]
        acc = cur
        for lag in range(1, w):
            acc = acc + ext_ref[POOL_HALO - lag:POOL_HALO - lag + tm, cols]
        count = jnp.minimum(pos + 1, w).astype(F32)
        mixed = acc / count - cur
        y = jnp.dot(mixed.astype(BF16), w_ref[gi], preferred_element_type=F32)
        o_ref[:, cols] = (y * sc_ref[:, cols]).astype(o_ref.dtype)


def _pool(pg, w_groups, scale, *, batch, seq, tm):
    t = pg.shape[0]
    ns = seq // tm
    return pl.pallas_call(
        functools.partial(_pool_kernel, tm=tm),
        grid=(batch, ns),
        in_specs=[
            pl.BlockSpec((tm, POOL_WIDTH), lambda b, s: (b * ns + s, 0)),
            pl.BlockSpec(w_groups.shape, lambda b, s: (0, 0, 0)),
            pl.BlockSpec((1, POOL_WIDTH), lambda b, s: (0, 0)),
        ],
        out_specs=pl.BlockSpec((tm, POOL_WIDTH), lambda b, s: (b * ns + s, 0)),
        out_shape=jax.ShapeDtypeStruct((t, POOL_WIDTH), BF16),
        scratch_shapes=[pltpu.VMEM((tm + POOL_HALO, POOL_WIDTH), F32)],
        compiler_params=_params("parallel", "arbitrary"),
        name="multiscale_pool",
    )(pg, w_groups, scale)


def _merge_kernel(a_ref, p_ref, wa_ref, wp_ref, ga_ref, gp_ref, o_ref):
    ya = jnp.dot(a_ref[...], wa_ref[...], preferred_element_type=F32)
    yp = jnp.dot(p_ref[...], wp_ref[...], preferred_element_type=F32)
    o_ref[...] = (ga_ref[...] * ya + gp_ref[...] * yp).astype(o_ref.dtype)


def _merge(attn, pooled, wa, wp, pg, *, d_model, tm, tn):
    t = attn.shape[0]
    ga_off = POOL_WIDTH // tn
    gp_off = (POOL_WIDTH + d_model) // tn
    return pl.pallas_call(
        _merge_kernel,
        grid=(t // tm, d_model // tn),
        in_specs=[
            pl.BlockSpec((tm, FOX_WIDTH), lambda i, j: (i, 0)),
            pl.BlockSpec((tm, POOL_WIDTH), lambda i, j: (i, 0)),
            pl.BlockSpec((FOX_WIDTH, tn), lambda i, j: (0, j)),
            pl.BlockSpec((POOL_WIDTH, tn), lambda i, j: (0, j)),
            pl.BlockSpec((tm, tn), lambda i, j: (i, ga_off + j)),
            pl.BlockSpec((tm, tn), lambda i, j: (i, gp_off + j)),
        ],
        out_specs=pl.BlockSpec((tm, tn), lambda i, j: (i, j)),
        out_shape=jax.ShapeDtypeStruct((t, d_model), BF16),
        compiler_params=_params("parallel", "arbitrary"),
        name="gated_merge",
    )(attn, pooled, wa, wp, pg, pg)


def _out_proj_kernel(m_ref, w_ref, x_ref, o_ref):
    o_ref[...] = x_ref[...] + jnp.dot(m_ref[...], w_ref[...], preferred_element_type=F32)


def _out_proj(merged, wo, x2, *, tm, tn):
    t, d = x2.shape
    return pl.pallas_call(
        _out_proj_kernel,
        grid=(t // tm, d // tn),
        in_specs=[
            pl.BlockSpec((tm, d), lambda i, j: (i, 0)),
            pl.BlockSpec((d, tn), lambda i, j: (0, j)),
            pl.BlockSpec((tm, tn), lambda i, j: (i, j)),
        ],
        out_specs=pl.BlockSpec((tm, tn), lambda i, j: (i, j)),
        out_shape=jax.ShapeDtypeStruct((t, d), F32),
        compiler_params=_params("parallel", "arbitrary"),
        name="out_proj_residual",
    )(merged, wo, x2)


def _topk_rows(s, val_ref, idx_ref, payload=None):
    n = s.shape[0]
    rows = lax.broadcasted_iota(jnp.int32, s.shape, 0)

    def step(i, s):
        m = jnp.max(s, axis=0, keepdims=True)
        am = jnp.min(jnp.where(s == m, rows, n), axis=0, keepdims=True)
        hit = rows == am
        val_ref[pl.ds(i, 1), :] = m
        if payload is None:
            idx_ref[pl.ds(i, 1), :] = am
        else:
            idx_ref[pl.ds(i, 1), :] = jnp.max(jnp.where(hit, payload, -1), axis=0, keepdims=True)
        return jnp.where(hit, -jnp.inf, s)

    lax.fori_loop(0, PEER_TOPK, step, s)


def _route_kernel(x_ref, g_ref, wq_ref, k1_ref, k2_ref, idx_ref, gate_ref,
                  q_ref, v1_ref, i1_ref, v2_ref, i2_ref, ts_ref, te_ref):
    xn = _rms(x_ref[...], g_ref[...]).astype(BF16)
    q_ref[...] = jnp.dot(xn, wq_ref[...], preferred_element_type=F32).astype(BF16)

    def head(h, carry):
        off = pl.multiple_of(h * (2 * PEER_HALF), 2 * PEER_HALF)
        q1 = q_ref[:, pl.ds(off, PEER_HALF)]
        q2 = q_ref[:, pl.ds(off + PEER_HALF, PEER_HALF)]
        dn = (((1,), (1,)), ((), ()))
        s1 = lax.dot_general(k1_ref[h], q1, dn, preferred_element_type=F32)
        s2 = lax.dot_general(k2_ref[h], q2, dn, preferred_element_type=F32)
        _topk_rows(s1, v1_ref, i1_ref)
        _topk_rows(s2, v2_ref, i2_ref)
        v2 = v2_ref[...]
        i2 = i2_ref[...]
        cand = jnp.concatenate(
            [v1_ref[a:a + 1, :] + v2 for a in range(PEER_TOPK)], axis=0)
        cidx = jnp.concatenate(
            [i1_ref[a:a + 1, :] * PEER_KEYS + i2 for a in range(PEER_TOPK)], axis=0)
        _topk_rows(cand, ts_ref, te_ref, payload=cidx)
        ts = ts_ref[...]
        e = jnp.exp(ts - jnp.max(ts, axis=0, keepdims=True))
        row0 = pl.multiple_of(h * PEER_TOPK, PEER_TOPK)
        gate_ref[pl.ds(row0, PEER_TOPK), :] = e / jnp.sum(e, axis=0, keepdims=True)
        idx_ref[pl.ds(row0, PEER_TOPK), :] = te_ref[...]
        return carry

    lax.fori_loop(0, PEER_HEADS, head, 0)


def _route(x1, g, wq, k1, k2, *, tm):
    t, d = x1.shape
    nq = wq.shape[1]
    return pl.pallas_call(
        _route_kernel,
        grid=(t // tm,),
        in_specs=[
            pl.BlockSpec((tm, d), lambda i: (i, 0)),
            pl.BlockSpec((1, d), lambda i: (0, 0)),
            pl.BlockSpec((d, nq), lambda i: (0, 0)),
            pl.BlockSpec(k1.shape, lambda i: (0, 0, 0)),
            pl.BlockSpec(k2.shape, lambda i: (0, 0, 0)),
        ],
        out_specs=[
            pl.BlockSpec((PEER_PAIRS, tm), lambda i: (0, i)),
            pl.BlockSpec((PEER_PAIRS, tm), lambda i: (0, i)),
        ],
        out_shape=[jax.ShapeDtypeStruct((PEER_PAIRS, t), jnp.int32),
                   jax.ShapeDtypeStruct((PEER_PAIRS, t), F32)],
        scratch_shapes=[
            pltpu.VMEM((tm, nq), BF16),
            pltpu.VMEM((PEER_TOPK, tm), F32), pltpu.VMEM((PEER_TOPK, tm), jnp.int32),
            pltpu.VMEM((PEER_TOPK, tm), F32), pltpu.VMEM((PEER_TOPK, tm), jnp.int32),
            pltpu.VMEM((PEER_TOPK, tm), F32), pltpu.VMEM((PEER_TOPK, tm), jnp.int32),
        ],
        compiler_params=_params("parallel"),
        name="peer_route",
    )(x1, g, wq, k1, k2)


def _pack_kernel(u_ref, v_ref, o_ref):
    ub = lax.bitcast_convert_type(u_ref[...].astype(BF16).astype(F32), jnp.uint32)
    vb = lax.bitcast_convert_type(v_ref[...].astype(BF16).astype(F32), jnp.uint32)
    o_ref[...] = ub | (vb >> 16)


def _pack_tables(u, v, *, te):
    e, d = u.shape
    return pl.pallas_call(
        _pack_kernel,
        grid=(e // te,),
        in_specs=[pl.BlockSpec((te, d), lambda i: (i, 0)),
                  pl.BlockSpec((te, d), lambda i: (i, 0))],
        out_specs=pl.BlockSpec((te, d), lambda i: (i, 0)),
        out_shape=jax.ShapeDtypeStruct((e, d), jnp.uint32),
        compiler_params=_params("parallel"),
        name="peer_pack_tables",
    )(u, v)


def _gelu_tanh(a):
    c = 0.7978845608028654
    return 0.5 * a * (1.0 + jnp.tanh(c * (a + 0.044715 * (a * a * a))))


def _peer_kernel(idx_ref, x_ref, g_ref, gate_ref, tab_ref, o_ref,
                 xn_ref, buf0, buf1, sem, *, tb):
    bufs = (buf0, buf1)
    xn_ref[...] = _rms(x_ref[...], g_ref[...])
    lane = lax.broadcasted_iota(jnp.int32, (PEER_PAIRS, tb), 1)

    def row_copy(e, slot, k):
        return pltpu.make_async_copy(tab_ref.at[pl.ds(e, 1), :],
                                     bufs[slot].at[pl.ds(k, 1), :], sem.at[slot])

    def issue(tok, slot):
        base = tok * PEER_PAIRS
        for k in range(PEER_PAIRS):
            row_copy(idx_ref[0, 0, base + k], slot, k).start()

    def wait(slot):
        pltpu.make_async_copy(tab_ref.at[pl.ds(0, PEER_PAIRS), :], bufs[slot], sem.at[slot]).wait()

    def compute(tok, slot):
        w = bufs[slot][...]
        u = lax.bitcast_convert_type(w & jnp.uint32(0xFFFF0000), F32)
        v = lax.bitcast_convert_type(w << 16, F32)
        a = jnp.sum(u * xn_ref[pl.ds(tok, 1), :], axis=-1, keepdims=True)
        gate = jnp.sum(jnp.where(lane == tok, gate_ref[...], 0.0), axis=-1, keepdims=True)
        hid = _gelu_tanh(a) * gate
        o_ref[pl.ds(tok, 1), :] = x_ref[pl.ds(tok, 1), :] + jnp.sum(hid * v, axis=0, keepdims=True)

    issue(0, 0)

    def pair(p, carry):
        t0 = 2 * p
        issue(t0 + 1, 1)
        wait(0)
        compute(t0, 0)

        @pl.when(t0 + 2 < tb)
        def _():
            issue(t0 + 2, 0)

        wait(1)
        compute(t0 + 1, 1)
        return carry

    lax.fori_loop(0, tb // 2, pair, 0)


def _peer(idx_blocks, x1, g, gate_t, table, *, tb):
    t, d = x1.shape
    return pl.pallas_call(
        functools.partial(_peer_kernel, tb=tb),
        grid=(t // tb,),
        in_specs=[
            pl.BlockSpec((1, 1, tb * PEER_PAIRS), lambda i: (i, 0, 0), memory_space=pltpu.SMEM),
            pl.BlockSpec((tb, d), lambda i: (i, 0)),
            pl.BlockSpec((1, d), lambda i: (0, 0)),
            pl.BlockSpec((PEER_PAIRS, tb), lambda i: (0, i)),
            pl.BlockSpec(memory_space=pl.ANY),
        ],
        out_specs=pl.BlockSpec((tb, d), lambda i: (i, 0)),
        out_shape=jax.ShapeDtypeStruct((t, d), F32),
        scratch_shapes=[
            pltpu.VMEM((tb, d), F32),
            pltpu.VMEM((PEER_PAIRS, d), jnp.uint32),
            pltpu.VMEM((PEER_PAIRS, d), jnp.uint32),
            pltpu.SemaphoreType.DMA((2,)),
        ],
        compiler_params=_params("arbitrary"),
        name="peer_experts",
    )(idx_blocks, x1, g, gate_t, table)


def _layer(x2, batch, seq, norm1_g, w_in, forget_bias, q_norm_g, k_norm_g, pool_group_w,
           pool_scale, w_branch_attn, w_branch_pool, w_out, norm2_g, peer_w_query,
           peer_sub_keys_1, peer_sub_keys_2, peer_expert_u, peer_expert_v):
    t, d = x2.shape
    qkv_cols = 3 * FOX_WIDTH
    fl_end = qkv_cols + FOX_HEADS
    tm = min(1024, t)
    tn = 512

    g1 = norm1_g.reshape(1, d)
    w_qkv = w_in[:, :qkv_cols].astype(BF16)
    w_fl = jnp.pad(w_in[:, qkv_cols:fl_end], ((0, 0), (0, LANES - FOX_HEADS))).astype(BF16)
    b_fl = jnp.pad(forget_bias.astype(F32), (0, LANES - FOX_HEADS)).reshape(1, LANES)
    w_pg = w_in[:, fl_end:].astype(BF16)

    gain = jnp.concatenate([
        jnp.tile(q_norm_g.astype(F32) * (HEAD_DIM ** -0.5), FOX_HEADS),
        jnp.tile(k_norm_g.astype(F32), FOX_HEADS),
        jnp.ones((FOX_WIDTH,), F32)]).reshape(1, qkv_cols)
    qkv = _norm_proj(x2, g1, w_qkv, gain, mode="qkv", n_first=2 * FOX_WIDTH // tn,
                     out_dtype=BF16, tm=tm, tn=tn)
    pg = _norm_proj(x2, g1, w_pg, jnp.ones((1, w_pg.shape[1]), F32), mode="pg",
                    n_first=POOL_WIDTH // tn, out_dtype=F32, tm=tm, tn=tn)
    ct = _logf_cumsum(x2, g1, w_fl, b_fl, batch=batch, seq=seq, tm=min(512, seq))
    ta = min(512, seq)
    attn = _attention(qkv, ct.reshape(FOX_HEADS, 1, t), batch=batch, seq=seq, tq=ta, tk=ta)
    pooled = _pool(pg, pool_group_w.astype(BF16), pool_scale.reshape(1, POOL_WIDTH).astype(F32),
                   batch=batch, seq=seq, tm=min(512, seq))
    merged = _merge(attn, pooled, w_branch_attn.astype(BF16), w_branch_pool.astype(BF16), pg,
                    d_model=d, tm=tm, tn=tn)
    x1 = _out_proj(merged, w_out.astype(BF16), x2, tm=tm, tn=tn)

    idx_t, gate_t = _route(x1, norm2_g.reshape(1, d), peer_w_query.astype(BF16),
                           peer_sub_keys_1.astype(BF16), peer_sub_keys_2.astype(BF16), tm=LANES)
    tb = LANES
    idx_blocks = idx_t.T.reshape(t // tb, 1, tb * PEER_PAIRS)
    table = _pack_tables(peer_expert_u, peer_expert_v, te=256)
    return _peer(idx_blocks, x1, norm2_g.reshape(1, d), gate_t, table, tb=tb)


def kernel(x, norm1_g, w_in, forget_bias, q_norm_g, k_norm_g, pool_group_w, pool_scale,
           w_branch_attn, w_branch_pool, w_out, norm2_g, peer_w_query, peer_sub_keys_1,
           peer_sub_keys_2, peer_expert_u, peer_expert_v):
    b, s, d = x.shape
    x2 = x.reshape(b * s, d)
    for l in range(norm1_g.shape[0]):
        x2 = _layer(x2, b, s, norm1_g[l], w_in[l], forget_bias[l], q_norm_g[l], k_norm_g[l],
                    pool_group_w[l], pool_scale[l], w_branch_attn[l], w_branch_pool[l], w_out[l],
                    norm2_g[l], peer_w_query[l], peer_sub_keys_1[l], peer_sub_keys_2[l],
                    peer_expert_u[l], peer_expert_v[l])
    return x2.reshape(b, s, d)
```

```python
import functools

import jax
import jax.numpy as jnp
from jax import lax
from jax.experimental import pallas as pl
from jax.experimental.pallas import tpu as pltpu

F32 = jnp.float32
BF16 = jnp.bfloat16

RMS_EPS = 1e-6
LOG2E = 1.4426950408889634
FOX_HEADS = 8
HEAD_DIM = 128
FOX_WIDTH = FOX_HEADS * HEAD_DIM
POOL_WINDOWS = (2, 4, 8, 16)
POOL_WIDTH = 1024
POOL_GROUP_DIM = POOL_WIDTH // len(POOL_WINDOWS)
POOL_HALO = 16
PEER_HEADS = 8
PEER_KEYS = 128
PEER_HALF = 128
PEER_TOPK = 16
PEER_PAIRS = PEER_HEADS * PEER_TOPK

LANES = 128
VMEM_LIMIT_BYTES = 56 * 1024 * 1024


def _params(*sem):
    return pltpu.CompilerParams(dimension_semantics=sem, vmem_limit_bytes=VMEM_LIMIT_BYTES)


def _rms(x, g):
    ms = jnp.mean(x * x, axis=-1, keepdims=True)
    return x * lax.rsqrt(ms + RMS_EPS) * g


def _norm_proj_kernel(x_ref, g_ref, w_ref, gain_ref, o_ref, h_ref, *, mode, n_first, tn):
    j = pl.program_id(1)

    @pl.when(j == 0)
    def _():
        h_ref[...] = _rms(x_ref[...], g_ref[...]).astype(BF16)

    y = jnp.dot(h_ref[...], w_ref[...], preferred_element_type=F32)

    if mode == "qkv":
        @pl.when(j < n_first)
        def _():
            for hh in range(tn // HEAD_DIM):
                sl = slice(hh * HEAD_DIM, (hh + 1) * HEAD_DIM)
                blk = y[:, sl]
                ms = jnp.mean(blk * blk, axis=-1, keepdims=True)
                o_ref[:, sl] = (blk * lax.rsqrt(ms + RMS_EPS) * gain_ref[:, sl]).astype(o_ref.dtype)

        @pl.when(j >= n_first)
        def _():
            o_ref[...] = y.astype(o_ref.dtype)
    else:
        @pl.when(j < n_first)
        def _():
            o_ref[...] = y.astype(o_ref.dtype)

        @pl.when(j >= n_first)
        def _():
            o_ref[...] = (1.0 / (1.0 + jnp.exp(-y))).astype(o_ref.dtype)


def _norm_proj(x2, g, w, gain, *, mode, n_first, out_dtype, tm, tn):
    t, d = x2.shape
    n = w.shape[1]
    kern = functools.partial(_norm_proj_kernel, mode=mode, n_first=n_first, tn=tn)
    return pl.pallas_call(
        kern,
        grid=(t // tm, n // tn),
        in_specs=[
            pl.BlockSpec((tm, d), lambda i, j: (i, 0)),
            pl.BlockSpec((1, d), lambda i, j: (0, 0)),
            pl.BlockSpec((d, tn), lambda i, j: (0, j)),
            pl.BlockSpec((1, tn), lambda i, j: (0, j)),
        ],
        out_specs=pl.BlockSpec((tm, tn), lambda i, j: (i, j)),
        out_shape=jax.ShapeDtypeStruct((t, n), out_dtype),
        scratch_shapes=[pltpu.VMEM((tm, d), BF16)],
        compiler_params=_params("parallel", "arbitrary"),
        name="norm_proj_" + mode,
    )(x2, g, w, gain)


def _logf_kernel(x_ref, g_ref, w_ref, b_ref, ct_ref, carry_ref, *, tm):
    s = pl.program_id(1)

    @pl.when(s == 0)
    def _():
        carry_ref[...] = jnp.zeros_like(carry_ref)

    h = _rms(x_ref[...], g_ref[...]).astype(BF16)
    fl = jnp.dot(h, w_ref[...], preferred_element_type=F32) + b_ref[...]
    lf = jnp.minimum(fl, 0.0) - jnp.log(1.0 + jnp.exp(-jnp.abs(fl)))
    r = lax.broadcasted_iota(jnp.int32, (tm, tm), 0)
    c = lax.broadcasted_iota(jnp.int32, (tm, tm), 1)
    tri = (r >= c).astype(BF16)
    p1 = lf.astype(BF16)
    r1 = lf - p1.astype(F32)
    p2 = r1.astype(BF16)
    p3 = (r1 - p2.astype(F32)).astype(BF16)
    cs = (jnp.dot(tri, p1, preferred_element_type=F32)
          + jnp.dot(tri, p2, preferred_element_type=F32)
          + jnp.dot(tri, p3, preferred_element_type=F32)) + carry_ref[...]
    carry_ref[...] = cs[tm - 1:tm, :]
    ct_ref[...] = (cs * LOG2E).T[:FOX_HEADS, :]


def _logf_cumsum(x2, g, w_fl, b_fl, *, batch, seq, tm):
    t, d = x2.shape
    ns = seq // tm
    return pl.pallas_call(
        functools.partial(_logf_kernel, tm=tm),
        grid=(batch, ns),
        in_specs=[
            pl.BlockSpec((tm, d), lambda b, s: (b * ns + s, 0)),
            pl.BlockSpec((1, d), lambda b, s: (0, 0)),
            pl.BlockSpec((d, LANES), lambda b, s: (0, 0)),
            pl.BlockSpec((1, LANES), lambda b, s: (0, 0)),
        ],
        out_specs=pl.BlockSpec((FOX_HEADS, tm), lambda b, s: (0, b * ns + s)),
        out_shape=jax.ShapeDtypeStruct((FOX_HEADS, t), F32),
        scratch_shapes=[pltpu.VMEM((1, LANES), F32)],
        compiler_params=_params("parallel", "arbitrary"),
        name="logf_cumsum",
    )(x2, g, w_fl, b_fl)


ATTN_HEADS_PER_STEP = 2


def _attn_kernel(q_ref, k_ref, v_ref, ck_ref, o_ref, m_ref, l_ref, acc_ref, *, tq, tk, nk):
    qi = pl.program_id(2)
    kj = pl.program_id(3)

    @pl.when(kj == 0)
    def _():
        m_ref[...] = jnp.full_like(m_ref, -jnp.inf)
        l_ref[...] = jnp.zeros_like(l_ref)
        acc_ref[...] = jnp.zeros_like(acc_ref)

    def update(on_diagonal):
        for hh in range(ATTN_HEADS_PER_STEP):
            sl = slice(hh * HEAD_DIM, (hh + 1) * HEAD_DIM)
            s = lax.dot_general(q_ref[:, sl], k_ref[:, sl], (((1,), (1,)), ((), ())),
                                preferred_element_type=F32)
            s = s - ck_ref[hh]
            if on_diagonal:
                rows = qi * tq + lax.broadcasted_iota(jnp.int32, (tq, tk), 0)
                cols = kj * tk + lax.broadcasted_iota(jnp.int32, (tq, tk), 1)
                s = jnp.where(rows >= cols, s, -jnp.inf)
            m_prev = m_ref[hh]
            m_new = jnp.maximum(m_prev, jnp.max(s, axis=-1, keepdims=True))
            alpha = jnp.exp2(m_prev - m_new)
            p = jnp.exp2(s - m_new)
            l_ref[hh] = alpha * l_ref[hh] + jnp.sum(p, axis=-1, keepdims=True)
            acc_ref[hh] = alpha * acc_ref[hh] + jnp.dot(p.astype(BF16), v_ref[:, sl],
                                                        preferred_element_type=F32)
            m_ref[hh] = m_new

    first_col = kj * tk
    last_row = qi * tq + (tq - 1)

    @pl.when(first_col + (tk - 1) <= qi * tq)
    def _():
        update(False)

    @pl.when((first_col + (tk - 1) > qi * tq) & (first_col <= last_row))
    def _():
        update(True)

    @pl.when(kj == nk - 1)
    def _():
        for hh in range(ATTN_HEADS_PER_STEP):
            sl = slice(hh * HEAD_DIM, (hh + 1) * HEAD_DIM)
            o_ref[:, sl] = (acc_ref[hh] / l_ref[hh]).astype(o_ref.dtype)


def _attention(qkv, ct3, *, batch, seq, tq, tk):
    t = qkv.shape[0]
    nq, nk = seq // tq, seq // tk
    hp = ATTN_HEADS_PER_STEP
    width = hp * HEAD_DIM
    n_hb = FOX_HEADS // hp

    def last_kv(qi):
        return (qi * tq + (tq - 1)) // tk

    return pl.pallas_call(
        functools.partial(_attn_kernel, tq=tq, tk=tk, nk=nk),
        grid=(batch, n_hb, nq, nk),
        in_specs=[
            pl.BlockSpec((tq, width), lambda b, h, qi, kj: (b * nq + qi, h)),
            pl.BlockSpec((tk, width),
                         lambda b, h, qi, kj: (b * nk + jnp.minimum(kj, last_kv(qi)), n_hb + h)),
            pl.BlockSpec((tk, width),
                         lambda b, h, qi, kj: (b * nk + jnp.minimum(kj, last_kv(qi)), 2 * n_hb + h)),
            pl.BlockSpec((hp, 1, tk),
                         lambda b, h, qi, kj: (h, 0, b * nk + jnp.minimum(kj, last_kv(qi)))),
        ],
        out_specs=pl.BlockSpec((tq, width), lambda b, h, qi, kj: (b * nq + qi, h)),
        out_shape=jax.ShapeDtypeStruct((t, FOX_WIDTH), BF16),
        scratch_shapes=[pltpu.VMEM((hp, tq, 1), F32), pltpu.VMEM((hp, tq, 1), F32),
                        pltpu.VMEM((hp, tq, HEAD_DIM), F32)],
        compiler_params=_params("parallel", "parallel", "parallel", "arbitrary"),
        name="fox_attention",
    )(qkv, qkv, qkv, ct3)


def _pool_kernel(p_ref, w_ref, sc_ref, o_ref, ext_ref, *, tm):
    s = pl.program_id(1)

    @pl.when(s == 0)
    def _():
        ext_ref[0:POOL_HALO, :] = jnp.zeros((POOL_HALO, POOL_WIDTH), F32)

    @pl.when(s > 0)
    def _():
        ext_ref[0:POOL_HALO, :] = ext_ref[tm:tm + POOL_HALO, :]

    ext_ref[POOL_HALO:POOL_HALO + tm, :] = p_ref[...]
    pos = s * tm + lax.broadcasted_iota(jnp.int32, (tm, 1), 0)
    for gi, w in enumerate(POOL_WINDOWS):
        cols = slice(gi * POOL_GROUP_DIM, (gi + 1) * POOL_GROUP_DIM)
        cur = ext_ref[POOL_HALO:POOL_HALO + tm, cols]
        acc = cur
        for lag in range(1, w):
            acc = acc + ext_ref[POOL_HALO - lag:POOL_HALO - lag + tm, cols]
        count = jnp.minimum(pos + 1, w).astype(F32)
        mixed = acc / count - cur
        y = jnp.dot(mixed.astype(BF16), w_ref[gi], preferred_element_type=F32)
        o_ref[:, cols] = (y * sc_ref[:, cols]).astype(o_ref.dtype)


def _pool(pg, w_groups, scale, *, batch, seq, tm):
    t = pg.shape[0]
    ns = seq // tm
    return pl.pallas_call(
        functools.partial(_pool_kernel, tm=tm),
        grid=(batch, ns),
        in_specs=[
            pl.BlockSpec((tm, POOL_WIDTH), lambda b, s: (b * ns + s, 0)),
            pl.BlockSpec(w_groups.shape, lambda b, s: (0, 0, 0)),
            pl.BlockSpec((1, POOL_WIDTH), lambda b, s: (0, 0)),
        ],
        out_specs=pl.BlockSpec((tm, POOL_WIDTH), lambda b, s: (b * ns + s, 0)),
        out_shape=jax.ShapeDtypeStruct((t, POOL_WIDTH), BF16),
        scratch_shapes=[pltpu.VMEM((tm + POOL_HALO, POOL_WIDTH), F32)],
        compiler_params=_params("parallel", "arbitrary"),
        name="multiscale_pool",
    )(pg, w_groups, scale)


def _merge_kernel(a_ref, p_ref, wa_ref, wp_ref, ga_ref, gp_ref, o_ref):
    ya = jnp.dot(a_ref[...], wa_ref[...], preferred_element_type=F32)
    yp = jnp.dot(p_ref[...], wp_ref[...], preferred_element_type=F32)
    o_ref[...] = (ga_ref[...] * ya + gp_ref[...] * yp).astype(o_ref.dtype)


def _merge(attn, pooled, wa, wp, pg, *, d_model, tm, tn):
    t = attn.shape[0]
    ga_off = POOL_WIDTH // tn
    gp_off = (POOL_WIDTH + d_model) // tn
    return pl.pallas_call(
        _merge_kernel,
        grid=(t // tm, d_model // tn),
        in_specs=[
            pl.BlockSpec((tm, FOX_WIDTH), lambda i, j: (i, 0)),
            pl.BlockSpec((tm, POOL_WIDTH), lambda i, j: (i, 0)),
            pl.BlockSpec((FOX_WIDTH, tn), lambda i, j: (0, j)),
            pl.BlockSpec((POOL_WIDTH, tn), lambda i, j: (0, j)),
            pl.BlockSpec((tm, tn), lambda i, j: (i, ga_off + j)),
            pl.BlockSpec((tm, tn), lambda i, j: (i, gp_off + j)),
        ],
        out_specs=pl.BlockSpec((tm, tn), lambda i, j: (i, j)),
        out_shape=jax.ShapeDtypeStruct((t, d_model), BF16),
        compiler_params=_params("parallel", "arbitrary"),
        name="gated_merge",
    )(attn, pooled, wa, wp, pg, pg)


def _out_proj_kernel(m_ref, w_ref, x_ref, o_ref):
    o_ref[...] = x_ref[...] + jnp.dot(m_ref[...], w_ref[...], preferred_element_type=F32)


def _out_proj(merged, wo, x2, *, tm, tn):
    t, d = x2.shape
    return pl.pallas_call(
        _out_proj_kernel,
        grid=(t // tm, d // tn),
        in_specs=[
            pl.BlockSpec((tm, d), lambda i, j: (i, 0)),
            pl.BlockSpec((d, tn), lambda i, j: (0, j)),
            pl.BlockSpec((tm, tn), lambda i, j: (i, j)),
        ],
        out_specs=pl.BlockSpec((tm, tn), lambda i, j: (i, j)),
        out_shape=jax.ShapeDtypeStruct((t, d), F32),
        compiler_params=_params("parallel", "arbitrary"),
        name="out_proj_residual",
    )(merged, wo, x2)


def _extract_max(s, order):
    m = jnp.max(s, axis=0, keepdims=True)
    am = jnp.min(jnp.where(s == m, order, jnp.iinfo(jnp.int32).max), axis=0, keepdims=True)
    return m, am, order == am


def _topk_two(sa, sb, va_ref, ia_ref, vb_ref, ib_ref):
    rows = lax.broadcasted_iota(jnp.int32, sa.shape, 0)

    def step(i, carry):
        sa, sb = carry
        ma, ama, hita = _extract_max(sa, rows)
        mb, amb, hitb = _extract_max(sb, rows)
        va_ref[pl.ds(i, 1), :] = ma
        ia_ref[pl.ds(i, 1), :] = ama
        vb_ref[pl.ds(i, 1), :] = mb
        ib_ref[pl.ds(i, 1), :] = amb
        return jnp.where(hita, -jnp.inf, sa), jnp.where(hitb, -jnp.inf, sb)

    lax.fori_loop(0, PEER_TOPK, step, (sa, sb))


def _candidates(v1_ref, i1_ref, v2_ref, i2_ref, tm):
    sub = lax.broadcasted_iota(jnp.int32, (8, tm), 0)
    vals, poss, idxs = [], [], []
    for a in range(8):
        nb = PEER_TOPK // (a + 1)
        for b0 in range(0, nb, 8):
            v = v1_ref[a:a + 1, :] + v2_ref[b0:b0 + 8, :]
            vals.append(v if b0 + 8 <= nb else jnp.where(sub + b0 < nb, v, -jnp.inf))
            poss.append(sub + (a * PEER_TOPK + b0))
            idxs.append(i1_ref[a:a + 1, :] * PEER_KEYS + i2_ref[b0:b0 + 8, :])
    vals.append(v1_ref[8:16, :] + v2_ref[0:1, :])
    poss.append((sub + 8) * PEER_TOPK)
    idxs.append(i1_ref[8:16, :] * PEER_KEYS + i2_ref[0:1, :])
    return (jnp.concatenate(vals, axis=0), jnp.concatenate(poss, axis=0),
            jnp.concatenate(idxs, axis=0))


def _topk_candidates(cand, pos, cidx, val_ref, idx_ref):
    def step(i, s):
        m, _, hit = _extract_max(s, pos)
        val_ref[pl.ds(i, 1), :] = m
        idx_ref[pl.ds(i, 1), :] = jnp.max(jnp.where(hit, cidx, -1), axis=0, keepdims=True)
        return jnp.where(hit, -jnp.inf, s)

    lax.fori_loop(0, PEER_TOPK, step, cand)


def _route_kernel(x_ref, g_ref, wq_ref, k1_ref, k2_ref, idx_ref, gate2_ref,
                  q_ref, v1_ref, i1_ref, v2_ref, i2_ref, ts_ref, te_ref, gate_ref, *, tm):
    xn = _rms(x_ref[...], g_ref[...]).astype(BF16)
    q_ref[...] = jnp.dot(xn, wq_ref[...], preferred_element_type=F32).astype(BF16)

    def head(h, carry):
        off = pl.multiple_of(h * (2 * PEER_HALF), 2 * PEER_HALF)
        q1 = q_ref[:, pl.ds(off, PEER_HALF)]
        q2 = q_ref[:, pl.ds(off + PEER_HALF, PEER_HALF)]
        dn = (((1,), (1,)), ((), ()))
        s1 = lax.dot_general(k1_ref[h], q1, dn, preferred_element_type=F32)
        s2 = lax.dot_general(k2_ref[h], q2, dn, preferred_element_type=F32)
        _topk_two(s1, s2, v1_ref, i1_ref, v2_ref, i2_ref)
        cand, pos, cidx = _candidates(v1_ref, i1_ref, v2_ref, i2_ref, tm)
        _topk_candidates(cand, pos, cidx, ts_ref, te_ref)
        ts = ts_ref[...]
        e = jnp.exp(ts - jnp.max(ts, axis=0, keepdims=True))
        row0 = pl.multiple_of(h * PEER_TOPK, PEER_TOPK)
        gate_ref[pl.ds(row0, PEER_TOPK), :] = e / jnp.sum(e, axis=0, keepdims=True)
        idx_ref[pl.ds(row0, PEER_TOPK), :] = te_ref[...]
        return carry

    lax.fori_loop(0, PEER_HEADS, head, 0)

    r = lax.broadcasted_iota(jnp.int32, (2 * PEER_PAIRS, PEER_PAIRS), 0)
    c = lax.broadcasted_iota(jnp.int32, (2 * PEER_PAIRS, PEER_PAIRS), 1)
    dup = ((r >> 1) == c).astype(BF16)
    gate = gate_ref[...]
    p1 = gate.astype(BF16)
    r1 = gate - p1.astype(F32)
    p2 = r1.astype(BF16)
    p3 = (r1 - p2.astype(F32)).astype(BF16)
    gate2_ref[...] = (jnp.dot(dup, p1, preferred_element_type=F32)
                      + jnp.dot(dup, p2, preferred_element_type=F32)
                      + jnp.dot(dup, p3, preferred_element_type=F32))


def _route(x1, g, wq, k1, k2, *, tm):
    t, d = x1.shape
    nq = wq.shape[1]
    return pl.pallas_call(
        functools.partial(_route_kernel, tm=tm),
        grid=(t // tm,),
        in_specs=[
            pl.BlockSpec((tm, d), lambda i: (i, 0)),
            pl.BlockSpec((1, d), lambda i: (0, 0)),
            pl.BlockSpec((d, nq), lambda i: (0, 0)),
            pl.BlockSpec(k1.shape, lambda i: (0, 0, 0)),
            pl.BlockSpec(k2.shape, lambda i: (0, 0, 0)),
        ],
        out_specs=[
            pl.BlockSpec((PEER_PAIRS, tm), lambda i: (0, i)),
            pl.BlockSpec((2 * PEER_PAIRS, tm), lambda i: (0, i)),
        ],
        out_shape=[jax.ShapeDtypeStruct((PEER_PAIRS, t), jnp.int32),
                   jax.ShapeDtypeStruct((2 * PEER_PAIRS, t), F32)],
        scratch_shapes=[
            pltpu.VMEM((tm, nq), BF16),
            pltpu.VMEM((PEER_TOPK, tm), F32), pltpu.VMEM((PEER_TOPK, tm), jnp.int32),
            pltpu.VMEM((PEER_TOPK, tm), F32), pltpu.VMEM((PEER_TOPK, tm), jnp.int32),
            pltpu.VMEM((PEER_TOPK, tm), F32), pltpu.VMEM((PEER_TOPK, tm), jnp.int32),
            pltpu.VMEM((PEER_PAIRS, tm), F32),
        ],
        compiler_params=_params("parallel"),
        name="peer_route",
    )(x1, g, wq, k1, k2)


def _pack_kernel(u_ref, v_ref, o_ref):
    half = u_ref.shape[1]
    o_ref[:, :half] = pltpu.bitcast(u_ref[...].astype(BF16), jnp.int32)
    o_ref[:, half:] = pltpu.bitcast(v_ref[...].astype(BF16), jnp.int32)


def _pack_tables(u, v, *, te):
    e, d = u.shape
    half = d // 2
    u2 = u.reshape(2 * e, half)
    v2 = v.reshape(2 * e, half)
    return pl.pallas_call(
        _pack_kernel,
        grid=(e // te,),
        in_specs=[pl.BlockSpec((2 * te, half), lambda i: (i, 0)),
                  pl.BlockSpec((2 * te, half), lambda i: (i, 0))],
        out_specs=pl.BlockSpec((te, d), lambda i: (i, 0)),
        out_shape=jax.ShapeDtypeStruct((e, d), jnp.int32),
        compiler_params=_params("parallel"),
        name="peer_pack_tables",
    )(u2, v2)


SUBLANES = 8
PEER_ROW_GROUPS = PEER_PAIRS // SUBLANES
PEER_NBUF = 4
PEER_AHEAD = 2


def _gelu_tanh(a):
    c = 0.7978845608028654
    return 0.5 * a * (1.0 + jnp.tanh(c * (a + 0.044715 * (a * a * a))))


def _peer_kernel(idx_ref, x_ref, g_ref, gate_ref, tab_ref, o_ref, xn_ref, *rest, tb):
    bufs, sem = rest[:PEER_NBUF], rest[PEER_NBUF]
    d = x_ref.shape[1]
    half = d // 2
    rows = 2 * SUBLANES
    xn_ref[...] = _rms(x_ref[...], g_ref[...])
    lane = lax.broadcasted_iota(jnp.int32, (PEER_ROW_GROUPS, rows, tb), 2)
    even2 = (lax.broadcasted_iota(jnp.int32, (rows, 1), 0) & 1) == 0
    even3 = (lax.broadcasted_iota(jnp.int32, (1, rows, LANES), 1) & 1) == 0

    def row_copy(e, slot, k):
        return pltpu.make_async_copy(
            tab_ref.at[e], bufs[slot].at[k // SUBLANES, pl.ds(k % SUBLANES, 1), :], sem.at[slot])

    def issue(tok, slot):
        base = tok * PEER_PAIRS
        for k in range(PEER_PAIRS):
            row_copy(idx_ref[0, 0, base + k], slot, k).start()

    def wait(slot):
        pltpu.make_async_copy(bufs[slot], bufs[slot], sem.at[slot]).wait()

    def compute(tok, slot):
        wf = pltpu.bitcast(bufs[slot][...], BF16).astype(F32)
        xalt = jnp.where(even2, xn_ref[pl.ds(tok, 1), 0:half], xn_ref[pl.ds(tok, 1), half:d])
        prod = wf[:, :, :half] * xalt[None]
        part = prod[:, :, 0:LANES]
        for c in range(1, half // LANES):
            part = part + prod[:, :, c * LANES:(c + 1) * LANES]
        pair = part + pltpu.roll(part, rows - 1, axis=1)
        pair = jnp.where(even3, pair, pltpu.roll(pair, 1, axis=1))
        a = jnp.sum(pair, axis=-1, keepdims=True)
        gate = jnp.sum(jnp.where(lane == tok, gate_ref[...], 0.0), axis=-1, keepdims=True)
        hid = _gelu_tanh(a) * gate
        s = jnp.sum(hid * wf[:, :, half:], axis=0)
        s8 = s[0:SUBLANES] + s[SUBLANES:rows]
        s8 = s8 + pltpu.roll(s8, 2, axis=0)
        s8 = s8 + pltpu.roll(s8, 4, axis=0)
        o_ref[pl.ds(tok, 1), 0:half] = x_ref[pl.ds(tok, 1), 0:half] + s8[0:1]
        o_ref[pl.ds(tok, 1), half:d] = x_ref[pl.ds(tok, 1), half:d] + s8[1:2]

    def step(tok, j, prefetch):
        wait(j)
        if prefetch:
            issue(tok + PEER_AHEAD, (j + PEER_AHEAD) % PEER_NBUF)
        compute(tok, j)

    for j in range(PEER_AHEAD):
        issue(j, j)

    def group(p, carry):
        for j in range(PEER_NBUF):
            step(p * PEER_NBUF + j, j, True)
        return carry

    n_groups = tb // PEER_NBUF
    lax.fori_loop(0, n_groups - 1, group, 0)
    for j in range(PEER_NBUF):
        step((n_groups - 1) * PEER_NBUF + j, j, j + PEER_AHEAD < PEER_NBUF)


def _peer(idx_blocks, x1, g, gate3, table, *, tb):
    t, d = x1.shape
    buf = pltpu.VMEM((PEER_ROW_GROUPS, SUBLANES, d), jnp.int32)
    return pl.pallas_call(
        functools.partial(_peer_kernel, tb=tb),
        grid=(t // tb,),
        in_specs=[
            pl.BlockSpec((1, 1, tb * PEER_PAIRS), lambda i: (i, 0, 0), memory_space=pltpu.SMEM),
            pl.BlockSpec((tb, d), lambda i: (i, 0)),
            pl.BlockSpec((1, d), lambda i: (0, 0)),
            pl.BlockSpec((PEER_ROW_GROUPS, 2 * SUBLANES, tb), lambda i: (0, 0, i)),
            pl.BlockSpec(memory_space=pl.ANY),
        ],
        out_specs=pl.BlockSpec((tb, d), lambda i: (i, 0)),
        out_shape=jax.ShapeDtypeStruct((t, d), F32),
        scratch_shapes=[pltpu.VMEM((tb, d), F32)] + [buf] * PEER_NBUF
                       + [pltpu.SemaphoreType.DMA((PEER_NBUF,))],
        compiler_params=_params("arbitrary"),
        name="peer_experts",
    )(idx_blocks, x1, g, gate3, table)


def _layer(x2, batch, seq, norm1_g, w_in, forget_bias, q_norm_g, k_norm_g, pool_group_w,
           pool_scale, w_branch_attn, w_branch_pool, w_out, norm2_g, peer_w_query,
           peer_sub_keys_1, peer_sub_keys_2, peer_expert_u, peer_expert_v):
    t, d = x2.shape
    qkv_cols = 3 * FOX_WIDTH
    fl_end = qkv_cols + FOX_HEADS
    tm = min(1024, t)
    tn = 512

    g1 = norm1_g.reshape(1, d)
    w_qkv = w_in[:, :qkv_cols].astype(BF16)
    w_fl = jnp.pad(w_in[:, qkv_cols:fl_end], ((0, 0), (0, LANES - FOX_HEADS))).astype(BF16)
    b_fl = jnp.pad(forget_bias.astype(F32), (0, LANES - FOX_HEADS)).reshape(1, LANES)
    w_pg = w_in[:, fl_end:].astype(BF16)

    gain = jnp.concatenate([
        jnp.tile(q_norm_g.astype(F32) * (HEAD_DIM ** -0.5 * LOG2E), FOX_HEADS),
        jnp.tile(k_norm_g.astype(F32), FOX_HEADS),
        jnp.ones((FOX_WIDTH,), F32)]).reshape(1, qkv_cols)
    qkv = _norm_proj(x2, g1, w_qkv, gain, mode="qkv", n_first=2 * FOX_WIDTH // tn,
                     out_dtype=BF16, tm=tm, tn=tn)
    pg = _norm_proj(x2, g1, w_pg, jnp.ones((1, w_pg.shape[1]), F32), mode="pg",
                    n_first=POOL_WIDTH // tn, out_dtype=F32, tm=tm, tn=tn)
    ct = _logf_cumsum(x2, g1, w_fl, b_fl, batch=batch, seq=seq, tm=min(512, seq))
    ta = min(512, seq)
    attn = _attention(qkv, ct.reshape(FOX_HEADS, 1, t), batch=batch, seq=seq, tq=ta, tk=ta)
    pooled = _pool(pg, pool_group_w.astype(BF16), pool_scale.reshape(1, POOL_WIDTH).astype(F32),
                   batch=batch, seq=seq, tm=min(512, seq))
    merged = _merge(attn, pooled, w_branch_attn.astype(BF16), w_branch_pool.astype(BF16), pg,
                    d_model=d, tm=tm, tn=tn)
    x1 = _out_proj(merged, w_out.astype(BF16), x2, tm=tm, tn=tn)

    idx_t, gate2 = _route(x1, norm2_g.reshape(1, d), peer_w_query.astype(BF16),
                          peer_sub_keys_1.astype(BF16), peer_sub_keys_2.astype(BF16), tm=LANES)
    tb = LANES
    idx_blocks = idx_t.T.reshape(t // tb, 1, tb * PEER_PAIRS)
    gate3 = gate2.reshape(PEER_ROW_GROUPS, 2 * SUBLANES, t)
    table = _pack_tables(peer_expert_u, peer_expert_v, te=256)
    table = table.reshape(table.shape[0], 1, d)
    return _peer(idx_blocks, x1, norm2_g.reshape(1, d), gate3, table, tb=tb)


def kernel(x, norm1_g, w_in, forget_bias, q_norm_g, k_norm_g, pool_group_w, pool_scale,
           w_branch_attn, w_branch_pool, w_out, norm2_g, peer_w_query, peer_sub_keys_1,
           peer_sub_keys_2, peer_expert_u, peer_expert_v):
    b, s, d = x.shape
    x2 = x.reshape(b * s, d)
    for l in range(norm1_g.shape[0]):
        x2 = _layer(x2, b, s, norm1_g[l], w_in[l], forget_bias[l], q_norm_g[l], k_norm_g[l],
                    pool_group_w[l], pool_scale[l], w_branch_attn[l], w_branch_pool[l], w_out[l],
                    norm2_g[l], peer_w_query[l], peer_sub_keys_1[l], peer_sub_keys_2[l],
                    peer_expert_u[l], peer_expert_v[l])
    return x2.reshape(b, s, d)
```

```python
import functools

import jax
import jax.numpy as jnp
from jax import lax
from jax.experimental import pallas as pl
from jax.experimental.pallas import tpu as pltpu

F32 = jnp.float32
BF16 = jnp.bfloat16

RMS_EPS = 1e-6
LOG2E = 1.4426950408889634
FOX_HEADS = 8
HEAD_DIM = 128
FOX_WIDTH = FOX_HEADS * HEAD_DIM
POOL_WINDOWS = (2, 4, 8, 16)
POOL_WIDTH = 1024
POOL_GROUP_DIM = POOL_WIDTH // len(POOL_WINDOWS)
POOL_HALO = 16
PEER_HEADS = 8
PEER_KEYS = 128
PEER_HALF = 128
PEER_TOPK = 16
PEER_PAIRS = PEER_HEADS * PEER_TOPK

LANES = 128
VMEM_LIMIT_BYTES = 56 * 1024 * 1024


def _params(*sem):
    return pltpu.CompilerParams(dimension_semantics=sem, vmem_limit_bytes=VMEM_LIMIT_BYTES)


def _rms(x, g):
    ms = jnp.mean(x * x, axis=-1, keepdims=True)
    return x * lax.rsqrt(ms + RMS_EPS) * g


def _norm_proj_kernel(x_ref, g_ref, w_ref, gain_ref, o_ref, h_ref, *, mode, n_first, tn):
    j = pl.program_id(1)

    @pl.when(j == 0)
    def _():
        h_ref[...] = _rms(x_ref[...], g_ref[...]).astype(BF16)

    y = jnp.dot(h_ref[...], w_ref[...], preferred_element_type=F32)

    if mode == "qk":
        for hh in range(tn // HEAD_DIM):
            sl = slice(hh * HEAD_DIM, (hh + 1) * HEAD_DIM)
            blk = y[:, sl]
            ms = jnp.mean(blk * blk, axis=-1, keepdims=True)
            o_ref[:, sl] = (blk * lax.rsqrt(ms + RMS_EPS) * gain_ref[:, sl]).astype(o_ref.dtype)
    elif mode == "vt":
        o_ref[...] = y.T.astype(o_ref.dtype)
    else:
        @pl.when(j < n_first)
        def _():
            o_ref[...] = y.astype(o_ref.dtype)

        @pl.when(j >= n_first)
        def _():
            o_ref[...] = (1.0 / (1.0 + jnp.exp(-y))).astype(o_ref.dtype)


def _norm_proj(x2, g, w, gain, *, mode, n_first, out_dtype, tm, tn):
    t, d = x2.shape
    n = w.shape[1]
    kern = functools.partial(_norm_proj_kernel, mode=mode, n_first=n_first, tn=tn)
    if mode == "vt":
        out_spec = pl.BlockSpec((tn, tm), lambda i, j: (j, i))
        out_shape = jax.ShapeDtypeStruct((n, t), out_dtype)
    else:
        out_spec = pl.BlockSpec((tm, tn), lambda i, j: (i, j))
        out_shape = jax.ShapeDtypeStruct((t, n), out_dtype)
    return pl.pallas_call(
        kern,
        grid=(t // tm, n // tn),
        in_specs=[
            pl.BlockSpec((tm, d), lambda i, j: (i, 0)),
            pl.BlockSpec((1, d), lambda i, j: (0, 0)),
            pl.BlockSpec((d, tn), lambda i, j: (0, j)),
            pl.BlockSpec((1, tn), lambda i, j: (0, j)),
        ],
        out_specs=out_spec,
        out_shape=out_shape,
        scratch_shapes=[pltpu.VMEM((tm, d), BF16)],
        compiler_params=_params("parallel", "arbitrary"),
        name="norm_proj_" + mode,
    )(x2, g, w, gain)


def _logf_kernel(x_ref, g_ref, w_ref, b_ref, ct_ref, carry_ref, *, tm):
    s = pl.program_id(1)

    @pl.when(s == 0)
    def _():
        carry_ref[...] = jnp.zeros_like(carry_ref)

    h = _rms(x_ref[...], g_ref[...]).astype(BF16)
    fl = jnp.dot(h, w_ref[...], preferred_element_type=F32) + b_ref[...]
    lf = jnp.minimum(fl, 0.0) - jnp.log(1.0 + jnp.exp(-jnp.abs(fl)))
    r = lax.broadcasted_iota(jnp.int32, (tm, tm), 0)
    c = lax.broadcasted_iota(jnp.int32, (tm, tm), 1)
    tri = (r >= c).astype(BF16)
    p1 = lf.astype(BF16)
    r1 = lf - p1.astype(F32)
    p2 = r1.astype(BF16)
    p3 = (r1 - p2.astype(F32)).astype(BF16)
    cs = (jnp.dot(tri, p1, preferred_element_type=F32)
          + jnp.dot(tri, p2, preferred_element_type=F32)
          + jnp.dot(tri, p3, preferred_element_type=F32)) + carry_ref[...]
    carry_ref[...] = cs[tm - 1:tm, :]
    c2 = cs * LOG2E
    hr = lax.broadcasted_iota(jnp.int32, (LANES, FOX_WIDTH), 0)
    hc = lax.broadcasted_iota(jnp.int32, (LANES, FOX_WIDTH), 1)
    spread = (hc // HEAD_DIM == hr).astype(BF16)
    q1 = c2.astype(BF16)
    s1 = c2 - q1.astype(F32)
    q2 = s1.astype(BF16)
    q3 = (s1 - q2.astype(F32)).astype(BF16)
    ct_ref[...] = (jnp.dot(q1, spread, preferred_element_type=F32)
                   + jnp.dot(q2, spread, preferred_element_type=F32)
                   + jnp.dot(q3, spread, preferred_element_type=F32))


def _logf_cumsum(x2, g, w_fl, b_fl, *, batch, seq, tm):
    t, d = x2.shape
    ns = seq // tm
    return pl.pallas_call(
        functools.partial(_logf_kernel, tm=tm),
        grid=(batch, ns),
        in_specs=[
            pl.BlockSpec((tm, d), lambda b, s: (b * ns + s, 0)),
            pl.BlockSpec((1, d), lambda b, s: (0, 0)),
            pl.BlockSpec((d, LANES), lambda b, s: (0, 0)),
            pl.BlockSpec((1, LANES), lambda b, s: (0, 0)),
        ],
        out_specs=pl.BlockSpec((tm, FOX_WIDTH), lambda b, s: (b * ns + s, 0)),
        out_shape=jax.ShapeDtypeStruct((t, FOX_WIDTH), F32),
        scratch_shapes=[pltpu.VMEM((1, LANES), F32)],
        compiler_params=_params("parallel", "arbitrary"),
        name="logf_cumsum",
    )(x2, g, w_fl, b_fl)


ATTN_HEADS_PER_STEP = 2


def _attn_kernel(q_ref, k_ref, vt_ref, ck_ref, o_ref, m_ref, l_ref, acc_ref, *, tq, tk, nk):
    qi = pl.program_id(2)
    kj = pl.program_id(3)

    @pl.when(kj == 0)
    def _():
        m_ref[...] = jnp.full_like(m_ref, -jnp.inf)
        l_ref[...] = jnp.zeros_like(l_ref)
        acc_ref[...] = jnp.zeros_like(acc_ref)

    def update(on_diagonal):
        for hh in range(ATTN_HEADS_PER_STEP):
            sl = slice(hh * HEAD_DIM, (hh + 1) * HEAD_DIM)
            s = lax.dot_general(k_ref[:, sl], q_ref[:, sl], (((1,), (1,)), ((), ())),
                                preferred_element_type=F32)
            s = s - jnp.concatenate([ck_ref[:, sl]] * (tq // LANES), axis=1)
            if on_diagonal:
                keys = kj * tk + lax.broadcasted_iota(jnp.int32, (tk, tq), 0)
                queries = qi * tq + lax.broadcasted_iota(jnp.int32, (tk, tq), 1)
                s = jnp.where(queries >= keys, s, -jnp.inf)
            m_prev = m_ref[hh]
            m_new = jnp.maximum(m_prev, jnp.max(s, axis=0, keepdims=True))
            alpha = jnp.exp2(m_prev - m_new)
            p = jnp.exp2(s - m_new)
            l_ref[hh] = alpha * l_ref[hh] + jnp.sum(p, axis=0, keepdims=True)
            acc_ref[hh] = alpha * acc_ref[hh] + jnp.dot(vt_ref[sl, :], p.astype(BF16),
                                                        preferred_element_type=F32)
            m_ref[hh] = m_new

    first_key = kj * tk
    last_query = qi * tq + (tq - 1)

    @pl.when(first_key + (tk - 1) <= qi * tq)
    def _():
        update(False)

    @pl.when((first_key + (tk - 1) > qi * tq) & (first_key <= last_query))
    def _():
        update(True)

    @pl.when(kj == nk - 1)
    def _():
        for hh in range(ATTN_HEADS_PER_STEP):
            sl = slice(hh * HEAD_DIM, (hh + 1) * HEAD_DIM)
            o_ref[:, sl] = (acc_ref[hh] / l_ref[hh]).T.astype(o_ref.dtype)


def _attention(qk, vt, ck, *, batch, seq, tq, tk):
    t = qk.shape[0]
    nq, nk = seq // tq, seq // tk
    hp = ATTN_HEADS_PER_STEP
    width = hp * HEAD_DIM
    n_hb = FOX_HEADS // hp

    def kv_block(b, qi, kj):
        return b * nk + jnp.minimum(kj, (qi * tq + (tq - 1)) // tk)

    return pl.pallas_call(
        functools.partial(_attn_kernel, tq=tq, tk=tk, nk=nk),
        grid=(batch, n_hb, nq, nk),
        in_specs=[
            pl.BlockSpec((tq, width), lambda b, h, qi, kj: (b * nq + qi, h)),
            pl.BlockSpec((tk, width), lambda b, h, qi, kj: (kv_block(b, qi, kj), n_hb + h)),
            pl.BlockSpec((width, tk), lambda b, h, qi, kj: (h, kv_block(b, qi, kj))),
            pl.BlockSpec((tk, width), lambda b, h, qi, kj: (kv_block(b, qi, kj), h)),
        ],
        out_specs=pl.BlockSpec((tq, width), lambda b, h, qi, kj: (b * nq + qi, h)),
        out_shape=jax.ShapeDtypeStruct((t, FOX_WIDTH), BF16),
        scratch_shapes=[pltpu.VMEM((hp, 1, tq), F32), pltpu.VMEM((hp, 1, tq), F32),
                        pltpu.VMEM((hp, HEAD_DIM, tq), F32)],
        compiler_params=_params("parallel", "parallel", "parallel", "arbitrary"),
        name="fox_attention",
    )(qk, qk, vt, ck)


def _pool_kernel(p_ref, w_ref, sc_ref, o_ref, ext_ref, *, tm):
    s = pl.program_id(1)

    @pl.when(s == 0)
    def _():
        ext_ref[0:POOL_HALO, :] = jnp.zeros((POOL_HALO, POOL_WIDTH), F32)

    @pl.when(s > 0)
    def _():
        ext_ref[0:POOL_HALO, :] = ext_ref[tm:tm + POOL_HALO, :]

    ext_ref[POOL_HALO:POOL_HALO + tm, :] = p_ref[...]
    pos = s * tm + lax.broadcasted_iota(jnp.int32, (tm, 1), 0)
    for gi, w in enumerate(POOL_WINDOWS):
        cols = slice(gi * POOL_GROUP_DIM, (gi + 1) * POOL_GROUP_DIM)
        cur = ext_ref[POOL_HALO:POOL_HALO + tm, cols]
        acc = cur
        for lag in range(1, w):
            acc = acc + ext_ref[POOL_HALO - lag:POOL_HALO - lag + tm, cols]
        count = jnp.minimum(pos + 1, w).astype(F32)
        mixed = acc / count - cur
        y = jnp.dot(mixed.astype(BF16), w_ref[gi], preferred_element_type=F32)
        o_ref[:, cols] = (y * sc_ref[:, cols]).astype(o_ref.dtype)


def _pool(pg, w_groups, scale, *, batch, seq, tm):
    t = pg.shape[0]
    ns = seq // tm
    return pl.pallas_call(
        functools.partial(_pool_kernel, tm=tm),
        grid=(batch, ns),
        in_specs=[
            pl.BlockSpec((tm, POOL_WIDTH), lambda b, s: (b * ns + s, 0)),
            pl.BlockSpec(w_groups.shape, lambda b, s: (0, 0, 0)),
            pl.BlockSpec((1, POOL_WIDTH), lambda b, s: (0, 0)),
        ],
        out_specs=pl.BlockSpec((tm, POOL_WIDTH), lambda b, s: (b * ns + s, 0)),
        out_shape=jax.ShapeDtypeStruct((t, POOL_WIDTH), BF16),
        scratch_shapes=[pltpu.VMEM((tm + POOL_HALO, POOL_WIDTH), F32)],
        compiler_params=_params("parallel", "arbitrary"),
        name="multiscale_pool",
    )(pg, w_groups, scale)


def _merge_kernel(a_ref, p_ref, wa_ref, wp_ref, ga_ref, gp_ref, o_ref):
    ya = jnp.dot(a_ref[...], wa_ref[...], preferred_element_type=F32)
    yp = jnp.dot(p_ref[...], wp_ref[...], preferred_element_type=F32)
    o_ref[...] = (ga_ref[...] * ya + gp_ref[...] * yp).astype(o_ref.dtype)


def _merge(attn, pooled, wa, wp, pg, *, d_model, tm, tn):
    t = attn.shape[0]
    ga_off = POOL_WIDTH // tn
    gp_off = (POOL_WIDTH + d_model) // tn
    return pl.pallas_call(
        _merge_kernel,
        grid=(t // tm, d_model // tn),
        in_specs=[
            pl.BlockSpec((tm, FOX_WIDTH), lambda i, j: (i, 0)),
            pl.BlockSpec((tm, POOL_WIDTH), lambda i, j: (i, 0)),
            pl.BlockSpec((FOX_WIDTH, tn), lambda i, j: (0, j)),
            pl.BlockSpec((POOL_WIDTH, tn), lambda i, j: (0, j)),
            pl.BlockSpec((tm, tn), lambda i, j: (i, ga_off + j)),
            pl.BlockSpec((tm, tn), lambda i, j: (i, gp_off + j)),
        ],
        out_specs=pl.BlockSpec((tm, tn), lambda i, j: (i, j)),
        out_shape=jax.ShapeDtypeStruct((t, d_model), BF16),
        compiler_params=_params("parallel", "arbitrary"),
        name="gated_merge",
    )(attn, pooled, wa, wp, pg, pg)


def _out_proj_kernel(m_ref, w_ref, x_ref, o_ref):
    o_ref[...] = x_ref[...] + jnp.dot(m_ref[...], w_ref[...], preferred_element_type=F32)


def _out_proj(merged, wo, x2, *, tm, tn):
    t, d = x2.shape
    return pl.pallas_call(
        _out_proj_kernel,
        grid=(t // tm, d // tn),
        in_specs=[
            pl.BlockSpec((tm, d), lambda i, j: (i, 0)),
            pl.BlockSpec((d, tn), lambda i, j: (0, j)),
            pl.BlockSpec((tm, tn), lambda i, j: (i, j)),
        ],
        out_specs=pl.BlockSpec((tm, tn), lambda i, j: (i, j)),
        out_shape=jax.ShapeDtypeStruct((t, d), F32),
        compiler_params=_params("parallel", "arbitrary"),
        name="out_proj_residual",
    )(merged, wo, x2)


def _extract_max(s, order):
    m = jnp.max(s, axis=0, keepdims=True)
    am = jnp.min(jnp.where(s == m, order, jnp.iinfo(jnp.int32).max), axis=0, keepdims=True)
    return m, am, order == am


def _topk_two(sa, sb, va_ref, ia_ref, vb_ref, ib_ref):
    rows = lax.broadcasted_iota(jnp.int32, sa.shape, 0)

    def step(i, carry):
        sa, sb = carry
        ma, ama, hita = _extract_max(sa, rows)
        mb, amb, hitb = _extract_max(sb, rows)
        va_ref[pl.ds(i, 1), :] = ma
        ia_ref[pl.ds(i, 1), :] = ama
        vb_ref[pl.ds(i, 1), :] = mb
        ib_ref[pl.ds(i, 1), :] = amb
        return jnp.where(hita, -jnp.inf, sa), jnp.where(hitb, -jnp.inf, sb)

    lax.fori_loop(0, PEER_TOPK, step, (sa, sb))


def _candidates(v1_ref, i1_ref, v2_ref, i2_ref, tm):
    sub = lax.broadcasted_iota(jnp.int32, (8, tm), 0)
    vals, poss, idxs = [], [], []
    for a in range(8):
        nb = PEER_TOPK // (a + 1)
        for b0 in range(0, nb, 8):
            v = v1_ref[a:a + 1, :] + v2_ref[b0:b0 + 8, :]
            vals.append(v if b0 + 8 <= nb else jnp.where(sub + b0 < nb, v, -jnp.inf))
            poss.append(sub + (a * PEER_TOPK + b0))
            idxs.append(i1_ref[a:a + 1, :] * PEER_KEYS + i2_ref[b0:b0 + 8, :])
    vals.append(v1_ref[8:16, :] + v2_ref[0:1, :])
    poss.append((sub + 8) * PEER_TOPK)
    idxs.append(i1_ref[8:16, :] * PEER_KEYS + i2_ref[0:1, :])
    return (jnp.concatenate(vals, axis=0), jnp.concatenate(poss, axis=0),
            jnp.concatenate(idxs, axis=0))


def _topk_candidates(cand, pos, cidx, val_ref, idx_ref):
    def step(i, s):
        m, _, hit = _extract_max(s, pos)
        val_ref[pl.ds(i, 1), :] = m
        idx_ref[pl.ds(i, 1), :] = jnp.max(jnp.where(hit, cidx, -1), axis=0, keepdims=True)
        return jnp.where(hit, -jnp.inf, s)

    lax.fori_loop(0, PEER_TOPK, step, cand)


def _route_kernel(x_ref, g_ref, wq_ref, k1_ref, k2_ref, idx_ref, gate2_ref,
                  q_ref, v1_ref, i1_ref, v2_ref, i2_ref, ts_ref, te_ref, gate_ref, *, tm):
    xn = _rms(x_ref[...], g_ref[...]).astype(BF16)
    q_ref[...] = jnp.dot(xn, wq_ref[...], preferred_element_type=F32).astype(BF16)

    def head(h, carry):
        off = pl.multiple_of(h * (2 * PEER_HALF), 2 * PEER_HALF)
        q1 = q_ref[:, pl.ds(off, PEER_HALF)]
        q2 = q_ref[:, pl.ds(off + PEER_HALF, PEER_HALF)]
        dn = (((1,), (1,)), ((), ()))
        s1 = lax.dot_general(k1_ref[h], q1, dn, preferred_element_type=F32)
        s2 = lax.dot_general(k2_ref[h], q2, dn, preferred_element_type=F32)
        _topk_two(s1, s2, v1_ref, i1_ref, v2_ref, i2_ref)
        cand, pos, cidx = _candidates(v1_ref, i1_ref, v2_ref, i2_ref, tm)
        _topk_candidates(cand, pos, cidx, ts_ref, te_ref)
        ts = ts_ref[...]
        e = jnp.exp(ts - jnp.max(ts, axis=0, keepdims=True))
        row0 = pl.multiple_of(h * PEER_TOPK, PEER_TOPK)
        gate_ref[pl.ds(row0, PEER_TOPK), :] = e / jnp.sum(e, axis=0, keepdims=True)
        idx_ref[pl.ds(row0, PEER_TOPK), :] = te_ref[...]
        return carry

    lax.fori_loop(0, PEER_HEADS, head, 0)

    r = lax.broadcasted_iota(jnp.int32, (2 * PEER_PAIRS, PEER_PAIRS), 0)
    c = lax.broadcasted_iota(jnp.int32, (2 * PEER_PAIRS, PEER_PAIRS), 1)
    dup = ((r >> 1) == c).astype(BF16)
    gate = gate_ref[...]
    p1 = gate.astype(BF16)
    r1 = gate - p1.astype(F32)
    p2 = r1.astype(BF16)
    p3 = (r1 - p2.astype(F32)).astype(BF16)
    gate2_ref[...] = (jnp.dot(dup, p1, preferred_element_type=F32)
                      + jnp.dot(dup, p2, preferred_element_type=F32)
                      + jnp.dot(dup, p3, preferred_element_type=F32))


def _route(x1, g, wq, k1, k2, *, tm):
    t, d = x1.shape
    nq = wq.shape[1]
    return pl.pallas_call(
        functools.partial(_route_kernel, tm=tm),
        grid=(t // tm,),
        in_specs=[
            pl.BlockSpec((tm, d), lambda i: (i, 0)),
            pl.BlockSpec((1, d), lambda i: (0, 0)),
            pl.BlockSpec((d, nq), lambda i: (0, 0)),
            pl.BlockSpec(k1.shape, lambda i: (0, 0, 0)),
            pl.BlockSpec(k2.shape, lambda i: (0, 0, 0)),
        ],
        out_specs=[
            pl.BlockSpec((PEER_PAIRS, tm), lambda i: (0, i)),
            pl.BlockSpec((2 * PEER_PAIRS, tm), lambda i: (0, i)),
        ],
        out_shape=[jax.ShapeDtypeStruct((PEER_PAIRS, t), jnp.int32),
                   jax.ShapeDtypeStruct((2 * PEER_PAIRS, t), F32)],
        scratch_shapes=[
            pltpu.VMEM((tm, nq), BF16),
            pltpu.VMEM((PEER_TOPK, tm), F32), pltpu.VMEM((PEER_TOPK, tm), jnp.int32),
            pltpu.VMEM((PEER_TOPK, tm), F32), pltpu.VMEM((PEER_TOPK, tm), jnp.int32),
            pltpu.VMEM((PEER_TOPK, tm), F32), pltpu.VMEM((PEER_TOPK, tm), jnp.int32),
            pltpu.VMEM((PEER_PAIRS, tm), F32),
        ],
        compiler_params=_params("parallel"),
        name="peer_route",
    )(x1, g, wq, k1, k2)


def _pack_kernel(u_ref, v_ref, o_ref):
    half = u_ref.shape[1]
    o_ref[:, :half] = pltpu.bitcast(u_ref[...].astype(BF16), jnp.int32)
    o_ref[:, half:] = pltpu.bitcast(v_ref[...].astype(BF16), jnp.int32)


def _pack_tables(u, v, *, te):
    e, d = u.shape
    half = d // 2
    u2 = u.reshape(2 * e, half)
    v2 = v.reshape(2 * e, half)
    return pl.pallas_call(
        _pack_kernel,
        grid=(e // te,),
        in_specs=[pl.BlockSpec((2 * te, half), lambda i: (i, 0)),
                  pl.BlockSpec((2 * te, half), lambda i: (i, 0))],
        out_specs=pl.BlockSpec((te, d), lambda i: (i, 0)),
        out_shape=jax.ShapeDtypeStruct((e, d), jnp.int32),
        compiler_params=_params("parallel"),
        name="peer_pack_tables",
    )(u2, v2)


SUBLANES = 8
PEER_ROW_GROUPS = PEER_PAIRS // SUBLANES
PEER_NBUF = 4
PEER_AHEAD = 3


def _gelu_tanh(a):
    c = 0.7978845608028654
    return 0.5 * a * (1.0 + jnp.tanh(c * (a + 0.044715 * (a * a * a))))


def _peer_kernel(idx_ref, x_ref, g_ref, gate_ref, tab_ref, o_ref, xn_ref, *rest, tb):
    bufs, sem = rest[:PEER_NBUF], rest[PEER_NBUF]
    d = x_ref.shape[1]
    half = d // 2
    rows = 2 * SUBLANES
    xn_ref[...] = _rms(x_ref[...], g_ref[...])
    lane = lax.broadcasted_iota(jnp.int32, (PEER_ROW_GROUPS, rows, tb), 2)
    even2 = (lax.broadcasted_iota(jnp.int32, (rows, 1), 0) & 1) == 0
    even3 = (lax.broadcasted_iota(jnp.int32, (1, rows, LANES), 1) & 1) == 0

    def row_copy(e, slot, k):
        return pltpu.make_async_copy(
            tab_ref.at[e], bufs[slot].at[k // SUBLANES, pl.ds(k % SUBLANES, 1), :], sem.at[slot])

    def issue(tok, slot):
        base = tok * PEER_PAIRS
        for k in range(PEER_PAIRS):
            row_copy(idx_ref[0, 0, base + k], slot, k).start(priority=k % 2)

    def wait(slot):
        pltpu.make_async_copy(bufs[slot], bufs[slot], sem.at[slot]).wait()

    def compute(tok, slot):
        wf = pltpu.bitcast(bufs[slot][...], BF16).astype(F32)
        xalt = jnp.where(even2, xn_ref[pl.ds(tok, 1), 0:half], xn_ref[pl.ds(tok, 1), half:d])
        prod = wf[:, :, :half] * xalt[None]
        part = prod[:, :, 0:LANES]
        for c in range(1, half // LANES):
            part = part + prod[:, :, c * LANES:(c + 1) * LANES]
        pair = part + pltpu.roll(part, rows - 1, axis=1)
        pair = jnp.where(even3, pair, pltpu.roll(pair, 1, axis=1))
        a = jnp.sum(pair, axis=-1, keepdims=True)
        gate = jnp.sum(jnp.where(lane == tok, gate_ref[...], 0.0), axis=-1, keepdims=True)
        hid = _gelu_tanh(a) * gate
        s = jnp.sum(hid * wf[:, :, half:], axis=0)
        s8 = s[0:SUBLANES] + s[SUBLANES:rows]
        s8 = s8 + pltpu.roll(s8, 2, axis=0)
        s8 = s8 + pltpu.roll(s8, 4, axis=0)
        o_ref[pl.ds(tok, 1), 0:half] = x_ref[pl.ds(tok, 1), 0:half] + s8[0:1]
        o_ref[pl.ds(tok, 1), half:d] = x_ref[pl.ds(tok, 1), half:d] + s8[1:2]

    def step(tok, j, prefetch):
        wait(j)
        if prefetch:
            issue(tok + PEER_AHEAD, (j + PEER_AHEAD) % PEER_NBUF)
        compute(tok, j)

    for j in range(PEER_AHEAD):
        issue(j, j)

    def group(p, carry):
        for j in range(PEER_NBUF):
            step(p * PEER_NBUF + j, j, True)
        return carry

    n_groups = tb // PEER_NBUF
    lax.fori_loop(0, n_groups - 1, group, 0)
    for j in range(PEER_NBUF):
        step((n_groups - 1) * PEER_NBUF + j, j, j + PEER_AHEAD < PEER_NBUF)


def _peer(idx_blocks, x1, g, gate3, table, *, tb):
    t, d = x1.shape
    buf = pltpu.VMEM((PEER_ROW_GROUPS, SUBLANES, d), jnp.int32)
    return pl.pallas_call(
        functools.partial(_peer_kernel, tb=tb),
        grid=(t // tb,),
        in_specs=[
            pl.BlockSpec((1, 1, tb * PEER_PAIRS), lambda i: (i, 0, 0), memory_space=pltpu.SMEM),
            pl.BlockSpec((tb, d), lambda i: (i, 0)),
            pl.BlockSpec((1, d), lambda i: (0, 0)),
            pl.BlockSpec((PEER_ROW_GROUPS, 2 * SUBLANES, tb), lambda i: (0, 0, i)),
            pl.BlockSpec(memory_space=pl.ANY),
        ],
        out_specs=pl.BlockSpec((tb, d), lambda i: (i, 0)),
        out_shape=jax.ShapeDtypeStruct((t, d), F32),
        scratch_shapes=[pltpu.VMEM((tb, d), F32)] + [buf] * PEER_NBUF
                       + [pltpu.SemaphoreType.DMA((PEER_NBUF,))],
        compiler_params=_params("arbitrary"),
        name="peer_experts",
    )(idx_blocks, x1, g, gate3, table)


def _layer(x2, batch, seq, norm1_g, w_in, forget_bias, q_norm_g, k_norm_g, pool_group_w,
           pool_scale, w_branch_attn, w_branch_pool, w_out, norm2_g, peer_w_query,
           peer_sub_keys_1, peer_sub_keys_2, peer_expert_u, peer_expert_v):
    t, d = x2.shape
    qkv_cols = 3 * FOX_WIDTH
    fl_end = qkv_cols + FOX_HEADS
    tm = min(1024, t)
    tn = 512

    g1 = norm1_g.reshape(1, d)
    w_qk = w_in[:, :2 * FOX_WIDTH].astype(BF16)
    w_v = w_in[:, 2 * FOX_WIDTH:qkv_cols].astype(BF16)
    w_fl = jnp.pad(w_in[:, qkv_cols:fl_end], ((0, 0), (0, LANES - FOX_HEADS))).astype(BF16)
    b_fl = jnp.pad(forget_bias.astype(F32), (0, LANES - FOX_HEADS)).reshape(1, LANES)
    w_pg = w_in[:, fl_end:].astype(BF16)

    gain = jnp.concatenate([
        jnp.tile(q_norm_g.astype(F32) * (HEAD_DIM ** -0.5 * LOG2E), FOX_HEADS),
        jnp.tile(k_norm_g.astype(F32), FOX_HEADS)]).reshape(1, 2 * FOX_WIDTH)
    qk = _norm_proj(x2, g1, w_qk, gain, mode="qk", n_first=0, out_dtype=BF16, tm=tm, tn=tn)
    vt = _norm_proj(x2, g1, w_v, jnp.ones((1, FOX_WIDTH), F32), mode="vt", n_first=0,
                    out_dtype=BF16, tm=tm, tn=tn)
    pg = _norm_proj(x2, g1, w_pg, jnp.ones((1, w_pg.shape[1]), F32), mode="pg",
                    n_first=POOL_WIDTH // tn, out_dtype=F32, tm=tm, tn=tn)
    ck = _logf_cumsum(x2, g1, w_fl, b_fl, batch=batch, seq=seq, tm=min(512, seq))
    ta = min(512, seq)
    attn = _attention(qk, vt, ck, batch=batch, seq=seq, tq=ta, tk=ta)
    pooled = _pool(pg, pool_group_w.astype(BF16), pool_scale.reshape(1, POOL_WIDTH).astype(F32),
                   batch=batch, seq=seq, tm=min(512, seq))
    merged = _merge(attn, pooled, w_branch_attn.astype(BF16), w_branch_pool.astype(BF16), pg,
                    d_model=d, tm=tm, tn=tn)
    x1 = _out_proj(merged, w_out.astype(BF16), x2, tm=tm, tn=tn)

    idx_t, gate2 = _route(x1, norm2_g.reshape(1, d), peer_w_query.astype(BF16),
                          peer_sub_keys_1.astype(BF16), peer_sub_keys_2.astype(BF16), tm=LANES)
    tb = LANES
    idx_blocks = idx_t.T.reshape(t // tb, 1, tb * PEER_PAIRS)
    gate3 = gate2.reshape(PEER_ROW_GROUPS, 2 * SUBLANES, t)
    table = _pack_tables(peer_expert_u, peer_expert_v, te=256)
    table = table.reshape(table.shape[0], 1, d)
    return _peer(idx_blocks, x1, norm2_g.reshape(1, d), gate3, table, tb=tb)


def kernel(x, norm1_g, w_in, forget_bias, q_norm_g, k_norm_g, pool_group_w, pool_scale,
           w_branch_attn, w_branch_pool, w_out, norm2_g, peer_w_query, peer_sub_keys_1,
           peer_sub_keys_2, peer_expert_u, peer_expert_v):
    b, s, d = x.shape
    x2 = x.reshape(b * s, d)
    for l in range(norm1_g.shape[0]):
        x2 = _layer(x2, b, s, norm1_g[l], w_in[l], forget_bias[l], q_norm_g[l], k_norm_g[l],
                    pool_group_w[l], pool_scale[l], w_branch_attn[l], w_branch_pool[l], w_out[l],
                    norm2_g[l], peer_w_query[l], peer_sub_keys_1[l], peer_sub_keys_2[l],
                    peer_expert_u[l], peer_expert_v[l])
    return x2.reshape(b, s, d)
```

```python
import functools

import jax
import jax.numpy as jnp
from jax import lax
from jax.experimental import pallas as pl
from jax.experimental.pallas import tpu as pltpu

F32 = jnp.float32
BF16 = jnp.bfloat16

RMS_EPS = 1e-6
FOX_HEADS = 8
HEAD_DIM = 128
FOX_WIDTH = FOX_HEADS * HEAD_DIM
POOL_WINDOWS = (2, 4, 8, 16)
POOL_WIDTH = 1024
POOL_GROUP_DIM = POOL_WIDTH // len(POOL_WINDOWS)
POOL_HALO = 16
PEER_HEADS = 8
PEER_KEYS = 128
PEER_HALF = 128
PEER_TOPK = 16
PEER_PAIRS = PEER_HEADS * PEER_TOPK

LANES = 128
VMEM_LIMIT_BYTES = 56 * 1024 * 1024


def _params(*sem):
    return pltpu.CompilerParams(dimension_semantics=sem, vmem_limit_bytes=VMEM_LIMIT_BYTES)


def _rms(x, g):
    ms = jnp.mean(x * x, axis=-1, keepdims=True)
    return x * lax.rsqrt(ms + RMS_EPS) * g


def _norm_proj_kernel(x_ref, g_ref, w_ref, gain_ref, o_ref, h_ref, *, mode, n_first, tn):
    j = pl.program_id(1)

    @pl.when(j == 0)
    def _():
        h_ref[...] = _rms(x_ref[...], g_ref[...]).astype(BF16)

    y = jnp.dot(h_ref[...], w_ref[...], preferred_element_type=F32)

    if mode == "qk":
        for hh in range(tn // HEAD_DIM):
            sl = slice(hh * HEAD_DIM, (hh + 1) * HEAD_DIM)
            blk = y[:, sl]
            ms = jnp.mean(blk * blk, axis=-1, keepdims=True)
            o_ref[:, sl] = (blk * lax.rsqrt(ms + RMS_EPS) * gain_ref[:, sl]).astype(o_ref.dtype)
    elif mode == "vt":
        o_ref[...] = y.T.astype(o_ref.dtype)
    else:
        @pl.when(j < n_first)
        def _():
            o_ref[...] = y.astype(o_ref.dtype)

        @pl.when(j >= n_first)
        def _():
            o_ref[...] = (1.0 / (1.0 + jnp.exp(-y))).astype(o_ref.dtype)


def _norm_proj(x2, g, w, gain, *, mode, n_first, out_dtype, tm, tn):
    t, d = x2.shape
    n = w.shape[1]
    kern = functools.partial(_norm_proj_kernel, mode=mode, n_first=n_first, tn=tn)
    if mode == "vt":
        out_spec = pl.BlockSpec((tn, tm), lambda i, j: (j, i))
        out_shape = jax.ShapeDtypeStruct((n, t), out_dtype)
    else:
        out_spec = pl.BlockSpec((tm, tn), lambda i, j: (i, j))
        out_shape = jax.ShapeDtypeStruct((t, n), out_dtype)
    return pl.pallas_call(
        kern,
        grid=(t // tm, n // tn),
        in_specs=[
            pl.BlockSpec((tm, d), lambda i, j: (i, 0)),
            pl.BlockSpec((1, d), lambda i, j: (0, 0)),
            pl.BlockSpec((d, tn), lambda i, j: (0, j)),
            pl.BlockSpec((1, tn), lambda i, j: (0, j)),
        ],
        out_specs=out_spec,
        out_shape=out_shape,
        scratch_shapes=[pltpu.VMEM((tm, d), BF16)],
        compiler_params=_params("parallel", "arbitrary"),
        name="norm_proj_" + mode,
    )(x2, g, w, gain)


def _logf_kernel(x_ref, g_ref, w_ref, b_ref, ct_ref, carry_ref, *, tm):
    s = pl.program_id(1)

    @pl.when(s == 0)
    def _():
        carry_ref[...] = jnp.zeros_like(carry_ref)

    h = _rms(x_ref[...], g_ref[...]).astype(BF16)
    fl = jnp.dot(h, w_ref[...], preferred_element_type=F32) + b_ref[...]
    lf = jnp.minimum(fl, 0.0) - jnp.log(1.0 + jnp.exp(-jnp.abs(fl)))
    r = lax.broadcasted_iota(jnp.int32, (tm, tm), 0)
    c = lax.broadcasted_iota(jnp.int32, (tm, tm), 1)
    tri = (r >= c).astype(BF16)
    p1 = lf.astype(BF16)
    r1 = lf - p1.astype(F32)
    p2 = r1.astype(BF16)
    p3 = (r1 - p2.astype(F32)).astype(BF16)
    cs = (jnp.dot(tri, p1, preferred_element_type=F32)
          + jnp.dot(tri, p2, preferred_element_type=F32)
          + jnp.dot(tri, p3, preferred_element_type=F32)) + carry_ref[...]
    carry_ref[...] = cs[tm - 1:tm, :]
    c2 = cs
    hr = lax.broadcasted_iota(jnp.int32, (LANES, FOX_WIDTH), 0)
    hc = lax.broadcasted_iota(jnp.int32, (LANES, FOX_WIDTH), 1)
    spread = (hc // HEAD_DIM == hr).astype(BF16)
    q1 = c2.astype(BF16)
    s1 = c2 - q1.astype(F32)
    q2 = s1.astype(BF16)
    q3 = (s1 - q2.astype(F32)).astype(BF16)
    ct_ref[...] = (jnp.dot(q1, spread, preferred_element_type=F32)
                   + jnp.dot(q2, spread, preferred_element_type=F32)
                   + jnp.dot(q3, spread, preferred_element_type=F32))


def _logf_cumsum(x2, g, w_fl, b_fl, *, batch, seq, tm):
    t, d = x2.shape
    ns = seq // tm
    return pl.pallas_call(
        functools.partial(_logf_kernel, tm=tm),
        grid=(batch, ns),
        in_specs=[
            pl.BlockSpec((tm, d), lambda b, s: (b * ns + s, 0)),
            pl.BlockSpec((1, d), lambda b, s: (0, 0)),
            pl.BlockSpec((d, LANES), lambda b, s: (0, 0)),
            pl.BlockSpec((1, LANES), lambda b, s: (0, 0)),
        ],
        out_specs=pl.BlockSpec((tm, FOX_WIDTH), lambda b, s: (b * ns + s, 0)),
        out_shape=jax.ShapeDtypeStruct((t, FOX_WIDTH), F32),
        scratch_shapes=[pltpu.VMEM((1, LANES), F32)],
        compiler_params=_params("parallel", "arbitrary"),
        name="logf_cumsum",
    )(x2, g, w_fl, b_fl)


ATTN_HEADS_PER_STEP = 2


def _attn_kernel(q_ref, k_ref, vt_ref, ck_ref, o_ref, m_ref, l_ref, acc_ref, *, tq, tk, nk):
    qi = pl.program_id(2)
    kj = pl.program_id(3)

    @pl.when(kj == 0)
    def _():
        m_ref[...] = jnp.full_like(m_ref, -jnp.inf)
        l_ref[...] = jnp.zeros_like(l_ref)
        acc_ref[...] = jnp.zeros_like(acc_ref)

    def update(on_diagonal):
        for hh in range(ATTN_HEADS_PER_STEP):
            sl = slice(hh * HEAD_DIM, (hh + 1) * HEAD_DIM)
            s = lax.dot_general(k_ref[:, sl], q_ref[:, sl], (((1,), (1,)), ((), ())),
                                preferred_element_type=F32)
            s = s - jnp.concatenate([ck_ref[:, sl]] * (tq // LANES), axis=1)
            if on_diagonal:
                keys = kj * tk + lax.broadcasted_iota(jnp.int32, (tk, tq), 0)
                queries = qi * tq + lax.broadcasted_iota(jnp.int32, (tk, tq), 1)
                s = jnp.where(queries >= keys, s, -jnp.inf)
            m_prev = m_ref[hh]
            m_new = jnp.maximum(m_prev, jnp.max(s, axis=0, keepdims=True))
            alpha = jnp.exp(m_prev - m_new)
            p = jnp.exp(s - m_new)
            l_ref[hh] = alpha * l_ref[hh] + jnp.sum(p, axis=0, keepdims=True)
            acc_ref[hh] = alpha * acc_ref[hh] + jnp.dot(vt_ref[sl, :], p.astype(BF16),
                                                        preferred_element_type=F32)
            m_ref[hh] = m_new

    first_key = kj * tk
    last_query = qi * tq + (tq - 1)

    @pl.when(first_key + (tk - 1) <= qi * tq)
    def _():
        update(False)

    @pl.when((first_key + (tk - 1) > qi * tq) & (first_key <= last_query))
    def _():
        update(True)

    @pl.when(kj == nk - 1)
    def _():
        for hh in range(ATTN_HEADS_PER_STEP):
            sl = slice(hh * HEAD_DIM, (hh + 1) * HEAD_DIM)
            o_ref[:, sl] = (acc_ref[hh] / l_ref[hh]).T.astype(o_ref.dtype)


def _attention(qk, vt, ck, *, batch, seq, tq, tk):
    t = qk.shape[0]
    nq, nk = seq // tq, seq // tk
    hp = ATTN_HEADS_PER_STEP
    width = hp * HEAD_DIM
    n_hb = FOX_HEADS // hp

    def kv_block(b, qi, kj):
        return b * nk + jnp.minimum(kj, (qi * tq + (tq - 1)) // tk)

    return pl.pallas_call(
        functools.partial(_attn_kernel, tq=tq, tk=tk, nk=nk),
        grid=(batch, n_hb, nq, nk),
        in_specs=[
            pl.BlockSpec((tq, width), lambda b, h, qi, kj: (b * nq + qi, h)),
            pl.BlockSpec((tk, width), lambda b, h, qi, kj: (kv_block(b, qi, kj), n_hb + h)),
            pl.BlockSpec((width, tk), lambda b, h, qi, kj: (h, kv_block(b, qi, kj))),
            pl.BlockSpec((tk, width), lambda b, h, qi, kj: (kv_block(b, qi, kj), h)),
        ],
        out_specs=pl.BlockSpec((tq, width), lambda b, h, qi, kj: (b * nq + qi, h)),
        out_shape=jax.ShapeDtypeStruct((t, FOX_WIDTH), BF16),
        scratch_shapes=[pltpu.VMEM((hp, 1, tq), F32), pltpu.VMEM((hp, 1, tq), F32),
                        pltpu.VMEM((hp, HEAD_DIM, tq), F32)],
        compiler_params=_params("parallel", "parallel", "parallel", "arbitrary"),
        name="fox_attention",
    )(qk, qk, vt, ck)


def _pool_kernel(p_ref, w_ref, sc_ref, o_ref, ext_ref, *, tm):
    s = pl.program_id(1)

    @pl.when(s == 0)
    def _():
        ext_ref[0:POOL_HALO, :] = jnp.zeros((POOL_HALO, POOL_WIDTH), F32)

    @pl.when(s > 0)
    def _():
        ext_ref[0:POOL_HALO, :] = ext_ref[tm:tm + POOL_HALO, :]

    ext_ref[POOL_HALO:POOL_HALO + tm, :] = p_ref[...]
    pos = s * tm + lax.broadcasted_iota(jnp.int32, (tm, 1), 0)
    for gi, w in enumerate(POOL_WINDOWS):
        cols = slice(gi * POOL_GROUP_DIM, (gi + 1) * POOL_GROUP_DIM)
        cur = ext_ref[POOL_HALO:POOL_HALO + tm, cols]
        acc = cur
        for lag in range(1, w):
            acc = acc + ext_ref[POOL_HALO - lag:POOL_HALO - lag + tm, cols]
        count = jnp.minimum(pos + 1, w).astype(F32)
        mixed = acc / count - cur
        y = jnp.dot(mixed.astype(BF16), w_ref[gi], preferred_element_type=F32)
        o_ref[:, cols] = (y * sc_ref[:, cols]).astype(o_ref.dtype)


def _pool(pg, w_groups, scale, *, batch, seq, tm):
    t = pg.shape[0]
    ns = seq // tm
    return pl.pallas_call(
        functools.partial(_pool_kernel, tm=tm),
        grid=(batch, ns),
        in_specs=[
            pl.BlockSpec((tm, POOL_WIDTH), lambda b, s: (b * ns + s, 0)),
            pl.BlockSpec(w_groups.shape, lambda b, s: (0, 0, 0)),
            pl.BlockSpec((1, POOL_WIDTH), lambda b, s: (0, 0)),
        ],
        out_specs=pl.BlockSpec((tm, POOL_WIDTH), lambda b, s: (b * ns + s, 0)),
        out_shape=jax.ShapeDtypeStruct((t, POOL_WIDTH), BF16),
        scratch_shapes=[pltpu.VMEM((tm + POOL_HALO, POOL_WIDTH), F32)],
        compiler_params=_params("parallel", "arbitrary"),
        name="multiscale_pool",
    )(pg, w_groups, scale)


def _merge_kernel(a_ref, p_ref, wa_ref, wp_ref, ga_ref, gp_ref, o_ref):
    ya = jnp.dot(a_ref[...], wa_ref[...], preferred_element_type=F32)
    yp = jnp.dot(p_ref[...], wp_ref[...], preferred_element_type=F32)
    o_ref[...] = (ga_ref[...] * ya + gp_ref[...] * yp).astype(o_ref.dtype)


def _merge(attn, pooled, wa, wp, pg, *, d_model, tm, tn):
    t = attn.shape[0]
    ga_off = POOL_WIDTH // tn
    gp_off = (POOL_WIDTH + d_model) // tn
    return pl.pallas_call(
        _merge_kernel,
        grid=(t // tm, d_model // tn),
        in_specs=[
            pl.BlockSpec((tm, FOX_WIDTH), lambda i, j: (i, 0)),
            pl.BlockSpec((tm, POOL_WIDTH), lambda i, j: (i, 0)),
            pl.BlockSpec((FOX_WIDTH, tn), lambda i, j: (0, j)),
            pl.BlockSpec((POOL_WIDTH, tn), lambda i, j: (0, j)),
            pl.BlockSpec((tm, tn), lambda i, j: (i, ga_off + j)),
            pl.BlockSpec((tm, tn), lambda i, j: (i, gp_off + j)),
        ],
        out_specs=pl.BlockSpec((tm, tn), lambda i, j: (i, j)),
        out_shape=jax.ShapeDtypeStruct((t, d_model), BF16),
        compiler_params=_params("parallel", "arbitrary"),
        name="gated_merge",
    )(attn, pooled, wa, wp, pg, pg)


def _out_proj_kernel(m_ref, w_ref, x_ref, o_ref):
    o_ref[...] = x_ref[...] + jnp.dot(m_ref[...], w_ref[...], preferred_element_type=F32)


def _out_proj(merged, wo, x2, *, tm, tn):
    t, d = x2.shape
    return pl.pallas_call(
        _out_proj_kernel,
        grid=(t // tm, d // tn),
        in_specs=[
            pl.BlockSpec((tm, d), lambda i, j: (i, 0)),
            pl.BlockSpec((d, tn), lambda i, j: (0, j)),
            pl.BlockSpec((tm, tn), lambda i, j: (i, j)),
        ],
        out_specs=pl.BlockSpec((tm, tn), lambda i, j: (i, j)),
        out_shape=jax.ShapeDtypeStruct((t, d), F32),
        compiler_params=_params("parallel", "arbitrary"),
        name="out_proj_residual",
    )(merged, wo, x2)


def _extract_max(s, order):
    m = jnp.max(s, axis=0, keepdims=True)
    am = jnp.min(jnp.where(s == m, order, jnp.iinfo(jnp.int32).max), axis=0, keepdims=True)
    return m, am, order == am


ROUTE_HEADS_PER_STEP = 2


def _topk_tiles(tiles, val_ref, idx_ref):
    rows = lax.broadcasted_iota(jnp.int32, tiles[0].shape, 0)

    def step(i, carry):
        out = []
        for n, s in enumerate(carry):
            m, am, hit = _extract_max(s, rows)
            val_ref[n, pl.ds(i, 1), :] = m
            idx_ref[n, pl.ds(i, 1), :] = am
            out.append(jnp.where(hit, -jnp.inf, s))
        return tuple(out)

    lax.fori_loop(0, PEER_TOPK, step, tuple(tiles))


def _candidates(v1, i1, v2, i2, tm):
    sub = lax.broadcasted_iota(jnp.int32, (8, tm), 0)
    vals, poss, idxs = [], [], []
    for a in range(8):
        nb = PEER_TOPK // (a + 1)
        for b0 in range(0, nb, 8):
            v = v1[a:a + 1, :] + v2[b0:b0 + 8, :]
            vals.append(v if b0 + 8 <= nb else jnp.where(sub + b0 < nb, v, -jnp.inf))
            poss.append(sub + (a * PEER_TOPK + b0))
            idxs.append(i1[a:a + 1, :] * PEER_KEYS + i2[b0:b0 + 8, :])
    vals.append(v1[8:16, :] + v2[0:1, :])
    poss.append((sub + 8) * PEER_TOPK)
    idxs.append(i1[8:16, :] * PEER_KEYS + i2[0:1, :])
    return (jnp.concatenate(vals, axis=0), jnp.concatenate(poss, axis=0),
            jnp.concatenate(idxs, axis=0))


def _topk_candidates(cands, pos, cidxs, val_ref, idx_ref):
    def step(i, carry):
        out = []
        for n, s in enumerate(carry):
            m, _, hit = _extract_max(s, pos)
            val_ref[n, pl.ds(i, 1), :] = m
            idx_ref[n, pl.ds(i, 1), :] = jnp.max(jnp.where(hit, cidxs[n], -1), axis=0,
                                                 keepdims=True)
            out.append(jnp.where(hit, -jnp.inf, s))
        return tuple(out)

    lax.fori_loop(0, PEER_TOPK, step, tuple(cands))


def _route_kernel(x_ref, g_ref, wq_ref, k1_ref, k2_ref, idx_ref, gate2_ref,
                  q_ref, v_ref, i_ref, ts_ref, te_ref, gate_ref, *, tm):
    xn = _rms(x_ref[...], g_ref[...]).astype(BF16)
    q_ref[...] = jnp.dot(xn, wq_ref[...], preferred_element_type=F32).astype(BF16)
    nh = ROUTE_HEADS_PER_STEP

    def heads(hg, carry):
        tiles = []
        for n in range(nh):
            h = hg * nh + n
            off = pl.multiple_of(h * (2 * PEER_HALF), 2 * PEER_HALF)
            q1 = q_ref[:, pl.ds(off, PEER_HALF)]
            q2 = q_ref[:, pl.ds(off + PEER_HALF, PEER_HALF)]
            dn = (((1,), (1,)), ((), ()))
            tiles.append(lax.dot_general(k1_ref[h], q1, dn, preferred_element_type=F32))
            tiles.append(lax.dot_general(k2_ref[h], q2, dn, preferred_element_type=F32))
        _topk_tiles(tiles, v_ref, i_ref)
        cands, cidxs = [], []
        for n in range(nh):
            cand, pos, cidx = _candidates(v_ref.at[2 * n], i_ref.at[2 * n],
                                          v_ref.at[2 * n + 1], i_ref.at[2 * n + 1], tm)
            cands.append(cand)
            cidxs.append(cidx)
        _topk_candidates(cands, pos, cidxs, ts_ref, te_ref)
        for n in range(nh):
            ts = ts_ref[n]
            e = jnp.exp(ts - jnp.max(ts, axis=0, keepdims=True))
            row0 = pl.multiple_of((hg * nh + n) * PEER_TOPK, PEER_TOPK)
            gate_ref[pl.ds(row0, PEER_TOPK), :] = e / jnp.sum(e, axis=0, keepdims=True)
            idx_ref[pl.ds(row0, PEER_TOPK), :] = te_ref[n]
        return carry

    lax.fori_loop(0, PEER_HEADS // nh, heads, 0)

    r = lax.broadcasted_iota(jnp.int32, (2 * PEER_PAIRS, PEER_PAIRS), 0)
    c = lax.broadcasted_iota(jnp.int32, (2 * PEER_PAIRS, PEER_PAIRS), 1)
    dup = ((r >> 1) == c).astype(BF16)
    gate = gate_ref[...]
    p1 = gate.astype(BF16)
    r1 = gate - p1.astype(F32)
    p2 = r1.astype(BF16)
    p3 = (r1 - p2.astype(F32)).astype(BF16)
    gate2_ref[...] = (jnp.dot(dup, p1, preferred_element_type=F32)
                      + jnp.dot(dup, p2, preferred_element_type=F32)
                      + jnp.dot(dup, p3, preferred_element_type=F32))


def _route(x1, g, wq, k1, k2, *, tm):
    t, d = x1.shape
    nq = wq.shape[1]
    nh = ROUTE_HEADS_PER_STEP
    return pl.pallas_call(
        functools.partial(_route_kernel, tm=tm),
        grid=(t // tm,),
        in_specs=[
            pl.BlockSpec((tm, d), lambda i: (i, 0)),
            pl.BlockSpec((1, d), lambda i: (0, 0)),
            pl.BlockSpec((d, nq), lambda i: (0, 0)),
            pl.BlockSpec(k1.shape, lambda i: (0, 0, 0)),
            pl.BlockSpec(k2.shape, lambda i: (0, 0, 0)),
        ],
        out_specs=[
            pl.BlockSpec((PEER_PAIRS, tm), lambda i: (0, i)),
            pl.BlockSpec((2 * PEER_PAIRS, tm), lambda i: (0, i)),
        ],
        out_shape=[jax.ShapeDtypeStruct((PEER_PAIRS, t), jnp.int32),
                   jax.ShapeDtypeStruct((2 * PEER_PAIRS, t), F32)],
        scratch_shapes=[
            pltpu.VMEM((tm, nq), BF16),
            pltpu.VMEM((2 * nh, PEER_TOPK, tm), F32), pltpu.VMEM((2 * nh, PEER_TOPK, tm), jnp.int32),
            pltpu.VMEM((nh, PEER_TOPK, tm), F32), pltpu.VMEM((nh, PEER_TOPK, tm), jnp.int32),
            pltpu.VMEM((PEER_PAIRS, tm), F32),
        ],
        compiler_params=_params("parallel"),
        name="peer_route",
    )(x1, g, wq, k1, k2)


def _pack_kernel(u_ref, v_ref, o_ref):
    te, d = u_ref.shape
    half = d // 2
    lo = jnp.concatenate([u_ref[:, :half], v_ref[:, :half]], axis=1).astype(BF16)
    hi = jnp.concatenate([u_ref[:, half:], v_ref[:, half:]], axis=1).astype(BF16)
    r = lax.broadcasted_iota(jnp.int32, (2 * te, te), 0)
    c = lax.broadcasted_iota(jnp.int32, (2 * te, te), 1)
    rows = (jnp.dot((r == 2 * c).astype(BF16), lo, preferred_element_type=F32)
            + jnp.dot((r == 2 * c + 1).astype(BF16), hi, preferred_element_type=F32))
    o_ref[...] = pltpu.bitcast(rows.astype(BF16), jnp.int32)


def _pack_tables(u, v, *, te):
    e, d = u.shape
    return pl.pallas_call(
        _pack_kernel,
        grid=(e // te,),
        in_specs=[pl.BlockSpec((te, d), lambda i: (i, 0)),
                  pl.BlockSpec((te, d), lambda i: (i, 0))],
        out_specs=pl.BlockSpec((te, d), lambda i: (i, 0)),
        out_shape=jax.ShapeDtypeStruct((e, d), jnp.int32),
        compiler_params=_params("parallel"),
        name="peer_pack_tables",
    )(u, v)


SUBLANES = 8
PEER_ROW_GROUPS = PEER_PAIRS // SUBLANES
PEER_NBUF = 8
PEER_AHEAD = 6


def _gelu_tanh(a):
    c = 0.7978845608028654
    return 0.5 * a * (1.0 + jnp.tanh(c * (a + 0.044715 * (a * a * a))))


def _peer_kernel(idx_ref, x_ref, g_ref, gate_ref, tab_ref, o_ref, xn_ref, *rest, tb):
    bufs, sem = rest[:PEER_NBUF], rest[PEER_NBUF]
    d = x_ref.shape[1]
    half = d // 2
    rows = 2 * SUBLANES
    xn_ref[...] = _rms(x_ref[...], g_ref[...])
    lane = lax.broadcasted_iota(jnp.int32, (PEER_ROW_GROUPS, rows, tb), 2)
    even2 = (lax.broadcasted_iota(jnp.int32, (rows, 1), 0) & 1) == 0
    even3 = (lax.broadcasted_iota(jnp.int32, (1, rows, LANES), 1) & 1) == 0

    def row_copy(e, slot, k):
        return pltpu.make_async_copy(
            tab_ref.at[e], bufs[slot].at[k // SUBLANES, pl.ds(k % SUBLANES, 1), :], sem.at[slot])

    def issue(tok, slot):
        base = tok * PEER_PAIRS
        for k in range(PEER_PAIRS):
            row_copy(idx_ref[0, 0, base + k], slot, k).start(priority=k % 2)

    def wait(slot):
        pltpu.make_async_copy(bufs[slot], bufs[slot], sem.at[slot]).wait()

    def compute(tok, slot):
        wf = pltpu.bitcast(bufs[slot][...], BF16).astype(F32)
        xalt = jnp.where(even2, xn_ref[pl.ds(tok, 1), 0:half], xn_ref[pl.ds(tok, 1), half:d])
        prod = wf[:, :, :half] * xalt[None]
        part = prod[:, :, 0:LANES]
        for c in range(1, half // LANES):
            part = part + prod[:, :, c * LANES:(c + 1) * LANES]
        pair = part + pltpu.roll(part, rows - 1, axis=1)
        pair = jnp.where(even3, pair, pltpu.roll(pair, 1, axis=1))
        a = jnp.sum(pair, axis=-1, keepdims=True)
        gate = jnp.sum(jnp.where(lane == tok, gate_ref[...], 0.0), axis=-1, keepdims=True)
        hid = _gelu_tanh(a) * gate
        s = jnp.sum(hid * wf[:, :, half:], axis=0)
        s8 = s[0:SUBLANES] + s[SUBLANES:rows]
        s8 = s8 + pltpu.roll(s8, 2, axis=0)
        s8 = s8 + pltpu.roll(s8, 4, axis=0)
        o_ref[pl.ds(tok, 1), 0:half] = x_ref[pl.ds(tok, 1), 0:half] + s8[0:1]
        o_ref[pl.ds(tok, 1), half:d] = x_ref[pl.ds(tok, 1), half:d] + s8[1:2]

    def step(tok, j, prefetch):
        wait(j)
        if prefetch:
            issue(tok + PEER_AHEAD, (j + PEER_AHEAD) % PEER_NBUF)
        compute(tok, j)

    for j in range(PEER_AHEAD):
        issue(j, j)

    def group(p, carry):
        for j in range(PEER_NBUF):
            step(p * PEER_NBUF + j, j, True)
        return carry

    n_groups = tb // PEER_NBUF
    lax.fori_loop(0, n_groups - 1, group, 0)
    for j in range(PEER_NBUF):
        step((n_groups - 1) * PEER_NBUF + j, j, j + PEER_AHEAD < PEER_NBUF)


def _peer(idx_blocks, x1, g, gate3, table, *, tb):
    t, d = x1.shape
    buf = pltpu.VMEM((PEER_ROW_GROUPS, SUBLANES, d), jnp.int32)
    return pl.pallas_call(
        functools.partial(_peer_kernel, tb=tb),
        grid=(t // tb,),
        in_specs=[
            pl.BlockSpec((1, 1, tb * PEER_PAIRS), lambda i: (i, 0, 0), memory_space=pltpu.SMEM),
            pl.BlockSpec((tb, d), lambda i: (i, 0)),
            pl.BlockSpec((1, d), lambda i: (0, 0)),
            pl.BlockSpec((PEER_ROW_GROUPS, 2 * SUBLANES, tb), lambda i: (0, 0, i)),
            pl.BlockSpec(memory_space=pl.ANY),
        ],
        out_specs=pl.BlockSpec((tb, d), lambda i: (i, 0)),
        out_shape=jax.ShapeDtypeStruct((t, d), F32),
        scratch_shapes=[pltpu.VMEM((tb, d), F32)] + [buf] * PEER_NBUF
                       + [pltpu.SemaphoreType.DMA((PEER_NBUF,))],
        compiler_params=_params("arbitrary"),
        name="peer_experts",
    )(idx_blocks, x1, g, gate3, table)


FUSED_NBUF = 16
FUSED_AHEAD = 14
ROUTE_CHUNKS = 32
CHUNKS_PER_PAIR = ROUTE_CHUNKS // (PEER_HEADS // ROUTE_HEADS_PER_STEP)
ITERS_PER_CHUNK = 2 * PEER_TOPK // CHUNKS_PER_PAIR
N_CAND = 80


def _peer_routed_kernel(xc_ref, xnext_ref, g_ref, wq_hbm, k1_ref, k2_ref, tab_ref, o_ref,
                        xn_ref, wq_ref, q_ref, tile_ref, v_ref, i_ref, cand_ref, cidx_ref,
                        ts_ref, te_ref, gate_acc, idx_next, gate_next, gate_cur, idx_smem,
                        *rest, tb, n_blocks):
    bufs, sem, wsem, ssem = rest[:FUSED_NBUF], rest[FUSED_NBUF], rest[FUSED_NBUF + 1], rest[FUSED_NBUF + 2]
    blk = pl.program_id(0)
    d = xc_ref.shape[1]
    half = d // 2
    rows = 2 * SUBLANES
    nh = ROUTE_HEADS_PER_STEP
    stage1_chunks = CHUNKS_PER_PAIR // 2

    def route_prepare(x_ref):
        xn = _rms(x_ref[...], g_ref[...]).astype(BF16)
        q_ref[...] = jnp.dot(xn, wq_ref[...], preferred_element_type=F32).astype(BF16)

    def stage1_init(hp):
        dn = (((1,), (1,)), ((), ()))
        for n in range(nh):
            h = hp * nh + n
            off = pl.multiple_of(h * (2 * PEER_HALF), 2 * PEER_HALF)
            tile_ref[2 * n] = lax.dot_general(k1_ref[h], q_ref[:, pl.ds(off, PEER_HALF)], dn,
                                              preferred_element_type=F32)
            tile_ref[2 * n + 1] = lax.dot_general(k2_ref[h], q_ref[:, pl.ds(off + PEER_HALF, PEER_HALF)],
                                                  dn, preferred_element_type=F32)

    def stage1_iters(i0):
        key_rows = lax.broadcasted_iota(jnp.int32, (PEER_KEYS, tb), 0)

        def step(i, carry):
            out = []
            for n, s in enumerate(carry):
                m, am, hit = _extract_max(s, key_rows)
                v_ref[n, pl.ds(i, 1), :] = m
                i_ref[n, pl.ds(i, 1), :] = am
                out.append(jnp.where(hit, -jnp.inf, s))
            return tuple(out)

        tiles = lax.fori_loop(i0, i0 + ITERS_PER_CHUNK, step,
                              tuple(tile_ref[n] for n in range(2 * nh)))
        for n in range(2 * nh):
            tile_ref[n] = tiles[n]

    def stage2_init():
        for n in range(nh):
            cand, _, cidx = _candidates(v_ref.at[2 * n], i_ref.at[2 * n],
                                        v_ref.at[2 * n + 1], i_ref.at[2 * n + 1], tb)
            cand_ref[n] = cand
            cidx_ref[n] = cidx

    def stage2_iters(i0):
        _, pos, _ = _candidates(v_ref.at[0], i_ref.at[0], v_ref.at[1], i_ref.at[1], tb)

        def step(i, carry):
            out = []
            for n, s in enumerate(carry):
                m, _, hit = _extract_max(s, pos)
                ts_ref[n, pl.ds(i, 1), :] = m
                te_ref[n, pl.ds(i, 1), :] = jnp.max(jnp.where(hit, cidx_ref[n], -1), axis=0,
                                                    keepdims=True)
                out.append(jnp.where(hit, -jnp.inf, s))
            return tuple(out)

        cands = lax.fori_loop(i0, i0 + ITERS_PER_CHUNK, step,
                              tuple(cand_ref[n] for n in range(nh)))
        for n in range(nh):
            cand_ref[n] = cands[n]

    def pair_finish(hp):
        for n in range(nh):
            ts = ts_ref[n]
            e = jnp.exp(ts - jnp.max(ts, axis=0, keepdims=True))
            row0 = pl.multiple_of((hp * nh + n) * PEER_TOPK, PEER_TOPK)
            gate_acc[pl.ds(row0, PEER_TOPK), :] = e / jnp.sum(e, axis=0, keepdims=True)
            idx_next[pl.ds(row0, PEER_TOPK), :] = te_ref[n]

    def route_chunk(c):
        hp = c // CHUNKS_PER_PAIR
        ph = c % CHUNKS_PER_PAIR

        @pl.when(ph == 0)
        def _():
            stage1_init(hp)

        @pl.when(ph < stage1_chunks)
        def _():
            stage1_iters(ph * ITERS_PER_CHUNK)

        @pl.when(ph == stage1_chunks)
        def _():
            stage2_init()

        @pl.when(ph >= stage1_chunks)
        def _():
            stage2_iters((ph - stage1_chunks) * ITERS_PER_CHUNK)

        @pl.when(ph == CHUNKS_PER_PAIR - 1)
        def _():
            pair_finish(hp)

    def route_finish():
        r = lax.broadcasted_iota(jnp.int32, (2 * PEER_PAIRS, PEER_PAIRS), 0)
        c = lax.broadcasted_iota(jnp.int32, (2 * PEER_PAIRS, PEER_PAIRS), 1)
        dup = ((r >> 1) == c).astype(BF16)
        gate = gate_acc[...]
        p1 = gate.astype(BF16)
        r1 = gate - p1.astype(F32)
        p2 = r1.astype(BF16)
        p3 = (r1 - p2.astype(F32)).astype(BF16)
        gate2 = (jnp.dot(dup, p1, preferred_element_type=F32)
                 + jnp.dot(dup, p2, preferred_element_type=F32)
                 + jnp.dot(dup, p3, preferred_element_type=F32))
        for gi in range(PEER_ROW_GROUPS):
            gate_next[gi] = gate2[gi * rows:(gi + 1) * rows, :]

    lane = lax.broadcasted_iota(jnp.int32, (PEER_ROW_GROUPS, rows, tb), 2)
    even2 = (lax.broadcasted_iota(jnp.int32, (rows, 1), 0) & 1) == 0
    even3 = (lax.broadcasted_iota(jnp.int32, (1, rows, LANES), 1) & 1) == 0

    def issue(tok, slot):
        for k in range(PEER_PAIRS):
            pltpu.make_async_copy(
                tab_ref.at[idx_smem[k, tok]],
                bufs[slot].at[k // SUBLANES, pl.ds(k % SUBLANES, 1), :],
                sem.at[slot]).start(priority=k % 2)

    def wait(slot):
        pltpu.make_async_copy(bufs[slot], bufs[slot], sem.at[slot]).wait()

    def compute(tok, slot):
        wf = pltpu.bitcast(bufs[slot][...], BF16).astype(F32)
        xalt = jnp.where(even2, xn_ref[pl.ds(tok, 1), 0:half], xn_ref[pl.ds(tok, 1), half:d])
        prod = wf[:, :, :half] * xalt[None]
        part = prod[:, :, 0:LANES]
        for c in range(1, half // LANES):
            part = part + prod[:, :, c * LANES:(c + 1) * LANES]
        pair = part + pltpu.roll(part, rows - 1, axis=1)
        pair = jnp.where(even3, pair, pltpu.roll(pair, 1, axis=1))
        a = jnp.sum(pair, axis=-1, keepdims=True)
        gate = jnp.sum(jnp.where(lane == tok, gate_cur[...], 0.0), axis=-1, keepdims=True)
        hid = _gelu_tanh(a) * gate
        s = jnp.sum(hid * wf[:, :, half:], axis=0)
        s8 = s[0:SUBLANES] + s[SUBLANES:rows]
        s8 = s8 + pltpu.roll(s8, 2, axis=0)
        s8 = s8 + pltpu.roll(s8, 4, axis=0)
        o_ref[pl.ds(tok, 1), 0:half] = xc_ref[pl.ds(tok, 1), 0:half] + s8[0:1]
        o_ref[pl.ds(tok, 1), half:d] = xc_ref[pl.ds(tok, 1), half:d] + s8[1:2]

    def step(tok, j, prefetch):
        wait(j)
        if prefetch:
            issue(tok + FUSED_AHEAD, (j + FUSED_AHEAD) % FUSED_NBUF)
        compute(tok, j)

    @pl.when(blk == 0)
    def _():
        cp = pltpu.make_async_copy(wq_hbm, wq_ref, wsem.at[0])
        cp.start()
        cp.wait()
        route_prepare(xc_ref)
        lax.fori_loop(0, ROUTE_CHUNKS, lambda c, carry: (route_chunk(c), carry)[1], 0)
        route_finish()

    cp = pltpu.make_async_copy(idx_next, idx_smem, ssem.at[0])
    cp.start()
    cp.wait()
    gate_cur[...] = gate_next[...]
    xn_ref[...] = _rms(xc_ref[...], g_ref[...])
    for j in range(FUSED_AHEAD):
        issue(j, j)

    has_next = blk + 1 < n_blocks

    @pl.when(has_next)
    def _():
        route_prepare(xnext_ref)

    tokens_per_chunk = tb // ROUTE_CHUNKS
    chunks_per_group = FUSED_NBUF // tokens_per_chunk

    def group_body(p, last):
        for j in range(FUSED_NBUF):
            tok = p * FUSED_NBUF + j
            step(tok, j, (not last) or (j + FUSED_AHEAD < FUSED_NBUF))
            if (j + 1) % tokens_per_chunk == 0:
                c = p * chunks_per_group + j // tokens_per_chunk

                @pl.when(has_next)
                def _():
                    route_chunk(c)

    n_groups = tb // FUSED_NBUF

    def group(p, carry):
        group_body(p, False)
        return carry

    lax.fori_loop(0, n_groups - 1, group, 0)
    group_body(n_groups - 1, True)

    @pl.when(has_next)
    def _():
        route_finish()


def _peer_routed(x1, g, wq, k1, k2, table, *, tb):
    t, d = x1.shape
    nq = wq.shape[1]
    n_blocks = t // tb
    nh = ROUTE_HEADS_PER_STEP
    buf = pltpu.VMEM((PEER_ROW_GROUPS, SUBLANES, d), jnp.int32)
    return pl.pallas_call(
        functools.partial(_peer_routed_kernel, tb=tb, n_blocks=n_blocks),
        grid=(n_blocks,),
        in_specs=[
            pl.BlockSpec((tb, d), lambda i: (i, 0)),
            pl.BlockSpec((tb, d), lambda i: (jnp.minimum(i + 1, n_blocks - 1), 0)),
            pl.BlockSpec((1, d), lambda i: (0, 0)),
            pl.BlockSpec(memory_space=pl.ANY),
            pl.BlockSpec(k1.shape, lambda i: (0, 0, 0)),
            pl.BlockSpec(k2.shape, lambda i: (0, 0, 0)),
            pl.BlockSpec(memory_space=pl.ANY),
        ],
        out_specs=pl.BlockSpec((tb, d), lambda i: (i, 0)),
        out_shape=jax.ShapeDtypeStruct((t, d), F32),
        scratch_shapes=[
            pltpu.VMEM((tb, d), F32),
            pltpu.VMEM((d, nq), BF16),
            pltpu.VMEM((tb, nq), BF16),
            pltpu.VMEM((2 * nh, PEER_KEYS, tb), F32),
            pltpu.VMEM((2 * nh, PEER_TOPK, tb), F32), pltpu.VMEM((2 * nh, PEER_TOPK, tb), jnp.int32),
            pltpu.VMEM((nh, N_CAND, tb), F32), pltpu.VMEM((nh, N_CAND, tb), jnp.int32),
            pltpu.VMEM((nh, PEER_TOPK, tb), F32), pltpu.VMEM((nh, PEER_TOPK, tb), jnp.int32),
            pltpu.VMEM((PEER_PAIRS, tb), F32),
            pltpu.VMEM((PEER_PAIRS, tb), jnp.int32),
            pltpu.VMEM((PEER_ROW_GROUPS, 2 * SUBLANES, tb), F32),
            pltpu.VMEM((PEER_ROW_GROUPS, 2 * SUBLANES, tb), F32),
            pltpu.SMEM((PEER_PAIRS, tb), jnp.int32),
        ] + [buf] * FUSED_NBUF + [pltpu.SemaphoreType.DMA((FUSED_NBUF,)),
                                  pltpu.SemaphoreType.DMA((1,)), pltpu.SemaphoreType.DMA((1,))],
        compiler_params=_params("arbitrary"),
        name="peer_routed_experts",
    )(x1, x1, g, wq, k1, k2, table)


def _layer(x2, batch, seq, norm1_g, w_in, forget_bias, q_norm_g, k_norm_g, pool_group_w,
           pool_scale, w_branch_attn, w_branch_pool, w_out, norm2_g, peer_w_query,
           peer_sub_keys_1, peer_sub_keys_2, peer_expert_u, peer_expert_v):
    t, d = x2.shape
    qkv_cols = 3 * FOX_WIDTH
    fl_end = qkv_cols + FOX_HEADS
    tm = min(1024, t)
    tn = 512

    g1 = norm1_g.reshape(1, d)
    w_qk = w_in[:, :2 * FOX_WIDTH].astype(BF16)
    w_v = w_in[:, 2 * FOX_WIDTH:qkv_cols].astype(BF16)
    w_fl = jnp.pad(w_in[:, qkv_cols:fl_end], ((0, 0), (0, LANES - FOX_HEADS))).astype(BF16)
    b_fl = jnp.pad(forget_bias.astype(F32), (0, LANES - FOX_HEADS)).reshape(1, LANES)
    w_pg = w_in[:, fl_end:].astype(BF16)

    gain = jnp.concatenate([
        jnp.tile(q_norm_g.astype(F32) * (HEAD_DIM ** -0.5), FOX_HEADS),
        jnp.tile(k_norm_g.astype(F32), FOX_HEADS)]).reshape(1, 2 * FOX_WIDTH)
    qk = _norm_proj(x2, g1, w_qk, gain, mode="qk", n_first=0, out_dtype=BF16, tm=tm, tn=tn)
    vt = _norm_proj(x2, g1, w_v, jnp.ones((1, FOX_WIDTH), F32), mode="vt", n_first=0,
                    out_dtype=BF16, tm=tm, tn=tn)
    pg = _norm_proj(x2, g1, w_pg, jnp.ones((1, w_pg.shape[1]), F32), mode="pg",
                    n_first=POOL_WIDTH // tn, out_dtype=F32, tm=tm, tn=tn)
    ck = _logf_cumsum(x2, g1, w_fl, b_fl, batch=batch, seq=seq, tm=min(512, seq))
    ta = min(512, seq)
    attn = _attention(qk, vt, ck, batch=batch, seq=seq, tq=ta, tk=ta)
    pooled = _pool(pg, pool_group_w.astype(BF16), pool_scale.reshape(1, POOL_WIDTH).astype(F32),
                   batch=batch, seq=seq, tm=min(512, seq))
    merged = _merge(attn, pooled, w_branch_attn.astype(BF16), w_branch_pool.astype(BF16), pg,
                    d_model=d, tm=tm, tn=tn)
    x1 = _out_proj(merged, w_out.astype(BF16), x2, tm=tm, tn=tn)

    table = _pack_tables(peer_expert_u, peer_expert_v, te=256)
    table = table.reshape(table.shape[0], 1, d)
    return _peer_routed(x1, norm2_g.reshape(1, d), peer_w_query.astype(BF16),
                        peer_sub_keys_1.astype(BF16), peer_sub_keys_2.astype(BF16), table, tb=LANES)


def kernel(x, norm1_g, w_in, forget_bias, q_norm_g, k_norm_g, pool_group_w, pool_scale,
           w_branch_attn, w_branch_pool, w_out, norm2_g, peer_w_query, peer_sub_keys_1,
           peer_sub_keys_2, peer_expert_u, peer_expert_v):
    b, s, d = x.shape
    x2 = x.reshape(b * s, d)
    for l in range(norm1_g.shape[0]):
        x2 = _layer(x2, b, s, norm1_g[l], w_in[l], forget_bias[l], q_norm_g[l], k_norm_g[l],
                    pool_group_w[l], pool_scale[l], w_branch_attn[l], w_branch_pool[l], w_out[l],
                    norm2_g[l], peer_w_query[l], peer_sub_keys_1[l], peer_sub_keys_2[l],
                    peer_expert_u[l], peer_expert_v[l])
    return x2.reshape(b, s, d)
```

```python
import functools

import jax
import jax.numpy as jnp
from jax import lax
from jax.experimental import pallas as pl
from jax.experimental.pallas import tpu as pltpu

F32 = jnp.float32
BF16 = jnp.bfloat16

RMS_EPS = 1e-6
FOX_HEADS = 8
HEAD_DIM = 128
FOX_WIDTH = FOX_HEADS * HEAD_DIM
POOL_WINDOWS = (2, 4, 8, 16)
POOL_WIDTH = 1024
POOL_GROUP_DIM = POOL_WIDTH // len(POOL_WINDOWS)
POOL_HALO = 16
PEER_HEADS = 8
PEER_KEYS = 128
PEER_HALF = 128
PEER_TOPK = 16
PEER_PAIRS = PEER_HEADS * PEER_TOPK

LANES = 128
VMEM_LIMIT_BYTES = 56 * 1024 * 1024


def _params(*sem):
    return pltpu.CompilerParams(dimension_semantics=sem, vmem_limit_bytes=VMEM_LIMIT_BYTES)


def _rms(x, g):
    ms = jnp.mean(x * x, axis=-1, keepdims=True)
    return x * lax.rsqrt(ms + RMS_EPS) * g


def _norm_proj_kernel(x_ref, g_ref, w_ref, gain_ref, o_ref, h_ref, *, mode, n_first, tn):
    j = pl.program_id(1)

    @pl.when(j == 0)
    def _():
        h_ref[...] = _rms(x_ref[...], g_ref[...]).astype(BF16)

    y = jnp.dot(h_ref[...], w_ref[...], preferred_element_type=F32)

    if mode == "qk":
        for hh in range(tn // HEAD_DIM):
            sl = slice(hh * HEAD_DIM, (hh + 1) * HEAD_DIM)
            blk = y[:, sl]
            ms = jnp.mean(blk * blk, axis=-1, keepdims=True)
            o_ref[:, sl] = (blk * lax.rsqrt(ms + RMS_EPS) * gain_ref[:, sl]).astype(o_ref.dtype)
    elif mode == "vt":
        o_ref[...] = y.T.astype(o_ref.dtype)
    else:
        gate = 1.0 / (1.0 + jnp.exp(-y))
        o_ref[...] = jnp.where(j >= n_first, gate, y).astype(o_ref.dtype)


def _norm_proj(x2, g, w, gain, *, mode, n_first, out_dtype, tm, tn):
    t, d = x2.shape
    n = w.shape[1]
    kern = functools.partial(_norm_proj_kernel, mode=mode, n_first=n_first, tn=tn)
    if mode == "vt":
        out_spec = pl.BlockSpec((tn, tm), lambda i, j: (j, i))
        out_shape = jax.ShapeDtypeStruct((n, t), out_dtype)
    else:
        out_spec = pl.BlockSpec((tm, tn), lambda i, j: (i, j))
        out_shape = jax.ShapeDtypeStruct((t, n), out_dtype)
    return pl.pallas_call(
        kern,
        grid=(t // tm, n // tn),
        in_specs=[
            pl.BlockSpec((tm, d), lambda i, j: (i, 0)),
            pl.BlockSpec((1, d), lambda i, j: (0, 0)),
            pl.BlockSpec((d, tn), lambda i, j: (0, j)),
            pl.BlockSpec((1, tn), lambda i, j: (0, j)),
        ],
        out_specs=out_spec,
        out_shape=out_shape,
        scratch_shapes=[pltpu.VMEM((tm, d), BF16)],
        compiler_params=_params("parallel", "arbitrary"),
        name="norm_proj_" + mode,
    )(x2, g, w, gain)


def _logf_kernel(x_ref, g_ref, w_ref, b_ref, ct_ref, carry_ref, *, tm):
    s = pl.program_id(1)

    @pl.when(s == 0)
    def _():
        carry_ref[...] = jnp.zeros_like(carry_ref)

    h = _rms(x_ref[...], g_ref[...]).astype(BF16)
    fl = jnp.dot(h, w_ref[...], preferred_element_type=F32) + b_ref[...]
    lf = jnp.minimum(fl, 0.0) - jnp.log(1.0 + jnp.exp(-jnp.abs(fl)))
    r = lax.broadcasted_iota(jnp.int32, (tm, tm), 0)
    c = lax.broadcasted_iota(jnp.int32, (tm, tm), 1)
    tri = (r >= c).astype(BF16)
    p1 = lf.astype(BF16)
    r1 = lf - p1.astype(F32)
    p2 = r1.astype(BF16)
    p3 = (r1 - p2.astype(F32)).astype(BF16)
    cs = (jnp.dot(tri, p1, preferred_element_type=F32)
          + jnp.dot(tri, p2, preferred_element_type=F32)
          + jnp.dot(tri, p3, preferred_element_type=F32)) + carry_ref[...]
    carry_ref[...] = cs[tm - 1:tm, :]
    c2 = cs
    hr = lax.broadcasted_iota(jnp.int32, (LANES, FOX_WIDTH), 0)
    hc = lax.broadcasted_iota(jnp.int32, (LANES, FOX_WIDTH), 1)
    spread = (hc // HEAD_DIM == hr).astype(BF16)
    q1 = c2.astype(BF16)
    s1 = c2 - q1.astype(F32)
    q2 = s1.astype(BF16)
    q3 = (s1 - q2.astype(F32)).astype(BF16)
    ct_ref[...] = (jnp.dot(q1, spread, preferred_element_type=F32)
                   + jnp.dot(q2, spread, preferred_element_type=F32)
                   + jnp.dot(q3, spread, preferred_element_type=F32))


def _logf_cumsum(x2, g, w_fl, b_fl, *, batch, seq, tm):
    t, d = x2.shape
    ns = seq // tm
    return pl.pallas_call(
        functools.partial(_logf_kernel, tm=tm),
        grid=(batch, ns),
        in_specs=[
            pl.BlockSpec((tm, d), lambda b, s: (b * ns + s, 0)),
            pl.BlockSpec((1, d), lambda b, s: (0, 0)),
            pl.BlockSpec((d, LANES), lambda b, s: (0, 0)),
            pl.BlockSpec((1, LANES), lambda b, s: (0, 0)),
        ],
        out_specs=pl.BlockSpec((tm, FOX_WIDTH), lambda b, s: (b * ns + s, 0)),
        out_shape=jax.ShapeDtypeStruct((t, FOX_WIDTH), F32),
        scratch_shapes=[pltpu.VMEM((1, LANES), F32)],
        compiler_params=_params("parallel", "arbitrary"),
        name="logf_cumsum",
    )(x2, g, w_fl, b_fl)


ATTN_HEADS_PER_STEP = 2


def _attn_kernel(q_ref, k_ref, vt_ref, ck_ref, o_ref, m_ref, l_ref, acc_ref, *, tq, tk, nk):
    qi = pl.program_id(2)
    kj = pl.program_id(3)

    @pl.when(kj == 0)
    def _():
        m_ref[...] = jnp.full_like(m_ref, -jnp.inf)
        l_ref[...] = jnp.zeros_like(l_ref)
        acc_ref[...] = jnp.zeros_like(acc_ref)

    def update(on_diagonal):
        for hh in range(ATTN_HEADS_PER_STEP):
            sl = slice(hh * HEAD_DIM, (hh + 1) * HEAD_DIM)
            s = lax.dot_general(k_ref[:, sl], q_ref[:, sl], (((1,), (1,)), ((), ())),
                                preferred_element_type=F32)
            s = s - jnp.concatenate([ck_ref[:, sl]] * (tq // LANES), axis=1)
            if on_diagonal:
                keys = kj * tk + lax.broadcasted_iota(jnp.int32, (tk, tq), 0)
                queries = qi * tq + lax.broadcasted_iota(jnp.int32, (tk, tq), 1)
                s = jnp.where(queries >= keys, s, -jnp.inf)
            m_prev = m_ref[hh]
            m_new = jnp.maximum(m_prev, jnp.max(s, axis=0, keepdims=True))
            alpha = jnp.exp(m_prev - m_new)
            p = jnp.exp(s - m_new)
            l_ref[hh] = alpha * l_ref[hh] + jnp.sum(p, axis=0, keepdims=True)
            acc_ref[hh] = alpha * acc_ref[hh] + jnp.dot(vt_ref[sl, :], p.astype(BF16),
                                                        preferred_element_type=F32)
            m_ref[hh] = m_new

    first_key = kj * tk
    last_query = qi * tq + (tq - 1)

    @pl.when(first_key + (tk - 1) <= qi * tq)
    def _():
        update(False)

    @pl.when((first_key + (tk - 1) > qi * tq) & (first_key <= last_query))
    def _():
        update(True)

    @pl.when(kj == nk - 1)
    def _():
        for hh in range(ATTN_HEADS_PER_STEP):
            sl = slice(hh * HEAD_DIM, (hh + 1) * HEAD_DIM)
            o_ref[:, sl] = (acc_ref[hh] / l_ref[hh]).T.astype(o_ref.dtype)


def _attention(qk, vt, ck, *, batch, seq, tq, tk):
    t = qk.shape[0]
    nq, nk = seq // tq, seq // tk
    hp = ATTN_HEADS_PER_STEP
    width = hp * HEAD_DIM
    n_hb = FOX_HEADS // hp

    def kv_block(b, qi, kj):
        return b * nk + jnp.minimum(kj, (qi * tq + (tq - 1)) // tk)

    return pl.pallas_call(
        functools.partial(_attn_kernel, tq=tq, tk=tk, nk=nk),
        grid=(batch, n_hb, nq, nk),
        in_specs=[
            pl.BlockSpec((tq, width), lambda b, h, qi, kj: (b * nq + qi, h)),
            pl.BlockSpec((tk, width), lambda b, h, qi, kj: (kv_block(b, qi, kj), n_hb + h)),
            pl.BlockSpec((width, tk), lambda b, h, qi, kj: (h, kv_block(b, qi, kj))),
            pl.BlockSpec((tk, width), lambda b, h, qi, kj: (kv_block(b, qi, kj), h)),
        ],
        out_specs=pl.BlockSpec((tq, width), lambda b, h, qi, kj: (b * nq + qi, h)),
        out_shape=jax.ShapeDtypeStruct((t, FOX_WIDTH), BF16),
        scratch_shapes=[pltpu.VMEM((hp, 1, tq), F32), pltpu.VMEM((hp, 1, tq), F32),
                        pltpu.VMEM((hp, HEAD_DIM, tq), F32)],
        compiler_params=_params("parallel", "parallel", "parallel", "arbitrary"),
        name="fox_attention",
    )(qk, qk, vt, ck)


def _pool_kernel(p_ref, w_ref, sc_ref, o_ref, ext_ref, *, tm):
    s = pl.program_id(1)

    @pl.when(s == 0)
    def _():
        ext_ref[0:POOL_HALO, :] = jnp.zeros((POOL_HALO, POOL_WIDTH), F32)

    @pl.when(s > 0)
    def _():
        ext_ref[0:POOL_HALO, :] = ext_ref[tm:tm + POOL_HALO, :]

    ext_ref[POOL_HALO:POOL_HALO + tm, :] = p_ref[...]
    pos = s * tm + lax.broadcasted_iota(jnp.int32, (tm, 1), 0)
    for gi, w in enumerate(POOL_WINDOWS):
        cols = slice(gi * POOL_GROUP_DIM, (gi + 1) * POOL_GROUP_DIM)
        cur = ext_ref[POOL_HALO:POOL_HALO + tm, cols]
        acc = cur
        for lag in range(1, w):
            acc = acc + ext_ref[POOL_HALO - lag:POOL_HALO - lag + tm, cols]
        count = jnp.minimum(pos + 1, w).astype(F32)
        mixed = acc / count - cur
        y = jnp.dot(mixed.astype(BF16), w_ref[gi], preferred_element_type=F32)
        o_ref[:, cols] = (y * sc_ref[:, cols]).astype(o_ref.dtype)


def _pool(pg, w_groups, scale, *, batch, seq, tm):
    t = pg.shape[0]
    ns = seq // tm
    return pl.pallas_call(
        functools.partial(_pool_kernel, tm=tm),
        grid=(batch, ns),
        in_specs=[
            pl.BlockSpec((tm, POOL_WIDTH), lambda b, s: (b * ns + s, 0)),
            pl.BlockSpec(w_groups.shape, lambda b, s: (0, 0, 0)),
            pl.BlockSpec((1, POOL_WIDTH), lambda b, s: (0, 0)),
        ],
        out_specs=pl.BlockSpec((tm, POOL_WIDTH), lambda b, s: (b * ns + s, 0)),
        out_shape=jax.ShapeDtypeStruct((t, POOL_WIDTH), BF16),
        scratch_shapes=[pltpu.VMEM((tm + POOL_HALO, POOL_WIDTH), F32)],
        compiler_params=_params("parallel", "arbitrary"),
        name="multiscale_pool",
    )(pg, w_groups, scale)


def _merge_kernel(a_ref, p_ref, wa_ref, wp_ref, ga_ref, gp_ref, o_ref):
    ya = jnp.dot(a_ref[...], wa_ref[...], preferred_element_type=F32)
    yp = jnp.dot(p_ref[...], wp_ref[...], preferred_element_type=F32)
    o_ref[...] = (ga_ref[...] * ya + gp_ref[...] * yp).astype(o_ref.dtype)


def _merge(attn, pooled, wa, wp, pg, *, d_model, tm, tn):
    t = attn.shape[0]
    ga_off = POOL_WIDTH // tn
    gp_off = (POOL_WIDTH + d_model) // tn
    return pl.pallas_call(
        _merge_kernel,
        grid=(t // tm, d_model // tn),
        in_specs=[
            pl.BlockSpec((tm, FOX_WIDTH), lambda i, j: (i, 0)),
            pl.BlockSpec((tm, POOL_WIDTH), lambda i, j: (i, 0)),
            pl.BlockSpec((FOX_WIDTH, tn), lambda i, j: (0, j)),
            pl.BlockSpec((POOL_WIDTH, tn), lambda i, j: (0, j)),
            pl.BlockSpec((tm, tn), lambda i, j: (i, ga_off + j)),
            pl.BlockSpec((tm, tn), lambda i, j: (i, gp_off + j)),
        ],
        out_specs=pl.BlockSpec((tm, tn), lambda i, j: (i, j)),
        out_shape=jax.ShapeDtypeStruct((t, d_model), BF16),
        compiler_params=_params("parallel", "arbitrary"),
        name="gated_merge",
    )(attn, pooled, wa, wp, pg, pg)


def _out_proj_kernel(m_ref, w_ref, x_ref, o_ref):
    o_ref[...] = x_ref[...] + jnp.dot(m_ref[...], w_ref[...], preferred_element_type=F32)


def _out_proj(merged, wo, x2, *, tm, tn):
    t, d = x2.shape
    return pl.pallas_call(
        _out_proj_kernel,
        grid=(t // tm, d // tn),
        in_specs=[
            pl.BlockSpec((tm, d), lambda i, j: (i, 0)),
            pl.BlockSpec((d, tn), lambda i, j: (0, j)),
            pl.BlockSpec((tm, tn), lambda i, j: (i, j)),
        ],
        out_specs=pl.BlockSpec((tm, tn), lambda i, j: (i, j)),
        out_shape=jax.ShapeDtypeStruct((t, d), F32),
        compiler_params=_params("parallel", "arbitrary"),
        name="out_proj_residual",
    )(merged, wo, x2)


def _extract_max(s, order):
    m = jnp.max(s, axis=0, keepdims=True)
    am = jnp.min(jnp.where(s == m, order, jnp.iinfo(jnp.int32).max), axis=0, keepdims=True)
    return m, am, order == am


ROUTE_HEADS_PER_STEP = 2


def _topk_tiles(tiles, val_ref, idx_ref):
    rows = lax.broadcasted_iota(jnp.int32, tiles[0].shape, 0)

    def step(i, carry):
        out = []
        for n, s in enumerate(carry):
            m, am, hit = _extract_max(s, rows)
            val_ref[n, pl.ds(i, 1), :] = m
            idx_ref[n, pl.ds(i, 1), :] = am
            out.append(jnp.where(hit, -jnp.inf, s))
        return tuple(out)

    lax.fori_loop(0, PEER_TOPK, step, tuple(tiles))


def _candidates(v1, i1, v2, i2, tm):
    sub = lax.broadcasted_iota(jnp.int32, (8, tm), 0)
    vals, poss, idxs = [], [], []
    for a in range(8):
        nb = PEER_TOPK // (a + 1)
        for b0 in range(0, nb, 8):
            v = v1[a:a + 1, :] + v2[b0:b0 + 8, :]
            vals.append(v if b0 + 8 <= nb else jnp.where(sub + b0 < nb, v, -jnp.inf))
            poss.append(sub + (a * PEER_TOPK + b0))
            idxs.append(i1[a:a + 1, :] * PEER_KEYS + i2[b0:b0 + 8, :])
    vals.append(v1[8:16, :] + v2[0:1, :])
    poss.append((sub + 8) * PEER_TOPK)
    idxs.append(i1[8:16, :] * PEER_KEYS + i2[0:1, :])
    return (jnp.concatenate(vals, axis=0), jnp.concatenate(poss, axis=0),
            jnp.concatenate(idxs, axis=0))


def _topk_candidates(cands, pos, cidxs, val_ref, idx_ref):
    def step(i, carry):
        out = []
        for n, s in enumerate(carry):
            m, _, hit = _extract_max(s, pos)
            val_ref[n, pl.ds(i, 1), :] = m
            idx_ref[n, pl.ds(i, 1), :] = jnp.max(jnp.where(hit, cidxs[n], -1), axis=0,
                                                 keepdims=True)
            out.append(jnp.where(hit, -jnp.inf, s))
        return tuple(out)

    lax.fori_loop(0, PEER_TOPK, step, tuple(cands))


def _route_kernel(x_ref, g_ref, wq_ref, k1_ref, k2_ref, idx_ref, gate2_ref,
                  q_ref, v_ref, i_ref, ts_ref, te_ref, gate_ref, *, tm):
    xn = _rms(x_ref[...], g_ref[...]).astype(BF16)
    q_ref[...] = jnp.dot(xn, wq_ref[...], preferred_element_type=F32).astype(BF16)
    nh = ROUTE_HEADS_PER_STEP

    def heads(hg, carry):
        tiles = []
        for n in range(nh):
            h = hg * nh + n
            off = pl.multiple_of(h * (2 * PEER_HALF), 2 * PEER_HALF)
            q1 = q_ref[:, pl.ds(off, PEER_HALF)]
            q2 = q_ref[:, pl.ds(off + PEER_HALF, PEER_HALF)]
            dn = (((1,), (1,)), ((), ()))
            tiles.append(lax.dot_general(k1_ref[h], q1, dn, preferred_element_type=F32))
            tiles.append(lax.dot_general(k2_ref[h], q2, dn, preferred_element_type=F32))
        _topk_tiles(tiles, v_ref, i_ref)
        cands, cidxs = [], []
        for n in range(nh):
            cand, pos, cidx = _candidates(v_ref.at[2 * n], i_ref.at[2 * n],
                                          v_ref.at[2 * n + 1], i_ref.at[2 * n + 1], tm)
            cands.append(cand)
            cidxs.append(cidx)
        _topk_candidates(cands, pos, cidxs, ts_ref, te_ref)
        for n in range(nh):
            ts = ts_ref[n]
            e = jnp.exp(ts - jnp.max(ts, axis=0, keepdims=True))
            row0 = pl.multiple_of((hg * nh + n) * PEER_TOPK, PEER_TOPK)
            gate_ref[pl.ds(row0, PEER_TOPK), :] = e / jnp.sum(e, axis=0, keepdims=True)
            idx_ref[pl.ds(row0, PEER_TOPK), :] = te_ref[n]
        return carry

    lax.fori_loop(0, PEER_HEADS // nh, heads, 0)

    r = lax.broadcasted_iota(jnp.int32, (2 * PEER_PAIRS, PEER_PAIRS), 0)
    c = lax.broadcasted_iota(jnp.int32, (2 * PEER_PAIRS, PEER_PAIRS), 1)
    dup = ((r >> 1) == c).astype(BF16)
    gate = gate_ref[...]
    p1 = gate.astype(BF16)
    r1 = gate - p1.astype(F32)
    p2 = r1.astype(BF16)
    p3 = (r1 - p2.astype(F32)).astype(BF16)
    gate2_ref[...] = (jnp.dot(dup, p1, preferred_element_type=F32)
                      + jnp.dot(dup, p2, preferred_element_type=F32)
                      + jnp.dot(dup, p3, preferred_element_type=F32))


def _route(x1, g, wq, k1, k2, *, tm):
    t, d = x1.shape
    nq = wq.shape[1]
    nh = ROUTE_HEADS_PER_STEP
    return pl.pallas_call(
        functools.partial(_route_kernel, tm=tm),
        grid=(t // tm,),
        in_specs=[
            pl.BlockSpec((tm, d), lambda i: (i, 0)),
            pl.BlockSpec((1, d), lambda i: (0, 0)),
            pl.BlockSpec((d, nq), lambda i: (0, 0)),
            pl.BlockSpec(k1.shape, lambda i: (0, 0, 0)),
            pl.BlockSpec(k2.shape, lambda i: (0, 0, 0)),
        ],
        out_specs=[
            pl.BlockSpec((PEER_PAIRS, tm), lambda i: (0, i)),
            pl.BlockSpec((2 * PEER_PAIRS, tm), lambda i: (0, i)),
        ],
        out_shape=[jax.ShapeDtypeStruct((PEER_PAIRS, t), jnp.int32),
                   jax.ShapeDtypeStruct((2 * PEER_PAIRS, t), F32)],
        scratch_shapes=[
            pltpu.VMEM((tm, nq), BF16),
            pltpu.VMEM((2 * nh, PEER_TOPK, tm), F32), pltpu.VMEM((2 * nh, PEER_TOPK, tm), jnp.int32),
            pltpu.VMEM((nh, PEER_TOPK, tm), F32), pltpu.VMEM((nh, PEER_TOPK, tm), jnp.int32),
            pltpu.VMEM((PEER_PAIRS, tm), F32),
        ],
        compiler_params=_params("parallel"),
        name="peer_route",
    )(x1, g, wq, k1, k2)


def _pack_kernel(u_ref, v_ref, o_ref):
    te, d = u_ref.shape
    half = d // 2
    lo = jnp.concatenate([u_ref[:, :half], v_ref[:, :half]], axis=1).astype(BF16)
    hi = jnp.concatenate([u_ref[:, half:], v_ref[:, half:]], axis=1).astype(BF16)
    r = lax.broadcasted_iota(jnp.int32, (2 * te, te), 0)
    c = lax.broadcasted_iota(jnp.int32, (2 * te, te), 1)
    rows = (jnp.dot((r == 2 * c).astype(BF16), lo, preferred_element_type=F32)
            + jnp.dot((r == 2 * c + 1).astype(BF16), hi, preferred_element_type=F32))
    o_ref[:, 0, :] = pltpu.bitcast(rows.astype(BF16), jnp.int32)


def _pack_tables(u, v, *, te):
    e, d = u.shape
    return pl.pallas_call(
        _pack_kernel,
        grid=(e // te,),
        in_specs=[pl.BlockSpec((te, d), lambda i: (i, 0)),
                  pl.BlockSpec((te, d), lambda i: (i, 0))],
        out_specs=pl.BlockSpec((te, 1, d), lambda i: (i, 0, 0)),
        out_shape=jax.ShapeDtypeStruct((e, 1, d), jnp.int32),
        compiler_params=_params("parallel"),
        name="peer_pack_tables",
    )(u, v)


SUBLANES = 8
PEER_ROW_GROUPS = PEER_PAIRS // SUBLANES
PEER_NBUF = 8
PEER_AHEAD = 6


def _gelu_tanh(a):
    c = 0.7978845608028654
    return 0.5 * a * (1.0 + jnp.tanh(c * (a + 0.044715 * (a * a * a))))


def _peer_kernel(idx_ref, x_ref, g_ref, gate_ref, tab_ref, o_ref, xn_ref, *rest, tb):
    bufs, sem = rest[:PEER_NBUF], rest[PEER_NBUF]
    d = x_ref.shape[1]
    half = d // 2
    rows = 2 * SUBLANES
    xn_ref[...] = _rms(x_ref[...], g_ref[...])
    lane = lax.broadcasted_iota(jnp.int32, (PEER_ROW_GROUPS, rows, tb), 2)
    even2 = (lax.broadcasted_iota(jnp.int32, (rows, 1), 0) & 1) == 0
    even3 = (lax.broadcasted_iota(jnp.int32, (1, rows, LANES), 1) & 1) == 0

    def row_copy(e, slot, k):
        return pltpu.make_async_copy(
            tab_ref.at[e], bufs[slot].at[k // SUBLANES, pl.ds(k % SUBLANES, 1), :], sem.at[slot])

    def issue(tok, slot):
        base = tok * PEER_PAIRS
        for k in range(PEER_PAIRS):
            row_copy(idx_ref[0, 0, base + k], slot, k).start(priority=k % 2)

    def wait(slot):
        pltpu.make_async_copy(bufs[slot], bufs[slot], sem.at[slot]).wait()

    def compute(tok, slot):
        wf = pltpu.bitcast(bufs[slot][...], BF16).astype(F32)
        xalt = jnp.where(even2, xn_ref[pl.ds(tok, 1), 0:half], xn_ref[pl.ds(tok, 1), half:d])
        prod = wf[:, :, :half] * xalt[None]
        part = prod[:, :, 0:LANES]
        for c in range(1, half // LANES):
            part = part + prod[:, :, c * LANES:(c + 1) * LANES]
        pair = part + pltpu.roll(part, rows - 1, axis=1)
        pair = jnp.where(even3, pair, pltpu.roll(pair, 1, axis=1))
        a = jnp.sum(pair, axis=-1, keepdims=True)
        gate = jnp.sum(jnp.where(lane == tok, gate_ref[...], 0.0), axis=-1, keepdims=True)
        hid = _gelu_tanh(a) * gate
        s = jnp.sum(hid * wf[:, :, half:], axis=0)
        s8 = s[0:SUBLANES] + s[SUBLANES:rows]
        s8 = s8 + pltpu.roll(s8, 2, axis=0)
        s8 = s8 + pltpu.roll(s8, 4, axis=0)
        o_ref[pl.ds(tok, 1), 0:half] = x_ref[pl.ds(tok, 1), 0:half] + s8[0:1]
        o_ref[pl.ds(tok, 1), half:d] = x_ref[pl.ds(tok, 1), half:d] + s8[1:2]

    def step(tok, j, prefetch):
        wait(j)
        if prefetch:
            issue(tok + PEER_AHEAD, (j + PEER_AHEAD) % PEER_NBUF)
        compute(tok, j)

    for j in range(PEER_AHEAD):
        issue(j, j)

    def group(p, carry):
        for j in range(PEER_NBUF):
            step(p * PEER_NBUF + j, j, True)
        return carry

    n_groups = tb // PEER_NBUF
    lax.fori_loop(0, n_groups - 1, group, 0)
    for j in range(PEER_NBUF):
        step((n_groups - 1) * PEER_NBUF + j, j, j + PEER_AHEAD < PEER_NBUF)


def _peer(idx_blocks, x1, g, gate3, table, *, tb):
    t, d = x1.shape
    buf = pltpu.VMEM((PEER_ROW_GROUPS, SUBLANES, d), jnp.int32)
    return pl.pallas_call(
        functools.partial(_peer_kernel, tb=tb),
        grid=(t // tb,),
        in_specs=[
            pl.BlockSpec((1, 1, tb * PEER_PAIRS), lambda i: (i, 0, 0), memory_space=pltpu.SMEM),
            pl.BlockSpec((tb, d), lambda i: (i, 0)),
            pl.BlockSpec((1, d), lambda i: (0, 0)),
            pl.BlockSpec((PEER_ROW_GROUPS, 2 * SUBLANES, tb), lambda i: (0, 0, i)),
            pl.BlockSpec(memory_space=pl.ANY),
        ],
        out_specs=pl.BlockSpec((tb, d), lambda i: (i, 0)),
        out_shape=jax.ShapeDtypeStruct((t, d), F32),
        scratch_shapes=[pltpu.VMEM((tb, d), F32)] + [buf] * PEER_NBUF
                       + [pltpu.SemaphoreType.DMA((PEER_NBUF,))],
        compiler_params=_params("arbitrary"),
        name="peer_experts",
    )(idx_blocks, x1, g, gate3, table)


def _layer(x2, batch, seq, norm1_g, w_in, forget_bias, q_norm_g, k_norm_g, pool_group_w,
           pool_scale, w_branch_attn, w_branch_pool, w_out, norm2_g, peer_w_query,
           peer_sub_keys_1, peer_sub_keys_2, peer_expert_u, peer_expert_v):
    t, d = x2.shape
    qkv_cols = 3 * FOX_WIDTH
    fl_end = qkv_cols + FOX_HEADS
    tm = min(1024, t)
    tn = 512

    g1 = norm1_g.reshape(1, d)
    w_qk = w_in[:, :2 * FOX_WIDTH].astype(BF16)
    w_v = w_in[:, 2 * FOX_WIDTH:qkv_cols].astype(BF16)
    w_fl = jnp.pad(w_in[:, qkv_cols:fl_end], ((0, 0), (0, LANES - FOX_HEADS))).astype(BF16)
    b_fl = jnp.pad(forget_bias.astype(F32), (0, LANES - FOX_HEADS)).reshape(1, LANES)
    w_pg = w_in[:, fl_end:].astype(BF16)

    gain = jnp.concatenate([
        jnp.tile(q_norm_g.astype(F32) * (HEAD_DIM ** -0.5), FOX_HEADS),
        jnp.tile(k_norm_g.astype(F32), FOX_HEADS)]).reshape(1, 2 * FOX_WIDTH)
    qk = _norm_proj(x2, g1, w_qk, gain, mode="qk", n_first=0, out_dtype=BF16, tm=tm, tn=tn)
    vt = _norm_proj(x2, g1, w_v, jnp.ones((1, FOX_WIDTH), F32), mode="vt", n_first=0,
                    out_dtype=BF16, tm=tm, tn=tn)
    pg = _norm_proj(x2, g1, w_pg, jnp.ones((1, w_pg.shape[1]), F32), mode="pg",
                    n_first=POOL_WIDTH // tn, out_dtype=F32, tm=tm, tn=tn)
    ck = _logf_cumsum(x2, g1, w_fl, b_fl, batch=batch, seq=seq, tm=min(512, seq))
    ta = min(512, seq)
    attn = _attention(qk, vt, ck, batch=batch, seq=seq, tq=ta, tk=ta)
    pooled = _pool(pg, pool_group_w.astype(BF16), pool_scale.reshape(1, POOL_WIDTH).astype(F32),
                   batch=batch, seq=seq, tm=min(512, seq))
    merged = _merge(attn, pooled, w_branch_attn.astype(BF16), w_branch_pool.astype(BF16), pg,
                    d_model=d, tm=tm, tn=tn)
    x1 = _out_proj(merged, w_out.astype(BF16), x2, tm=tm, tn=tn)

    idx_t, gate2 = _route(x1, norm2_g.reshape(1, d), peer_w_query.astype(BF16),
                          peer_sub_keys_1.astype(BF16), peer_sub_keys_2.astype(BF16), tm=LANES)
    tb = LANES
    idx_blocks = idx_t.T.reshape(t // tb, 1, tb * PEER_PAIRS)
    gate3 = gate2.reshape(PEER_ROW_GROUPS, 2 * SUBLANES, t)
    table = _pack_tables(peer_expert_u, peer_expert_v, te=256)
    return _peer(idx_blocks, x1, norm2_g.reshape(1, d), gate3, table, tb=tb)


def kernel(x, norm1_g, w_in, forget_bias, q_norm_g, k_norm_g, pool_group_w, pool_scale,
           w_branch_attn, w_branch_pool, w_out, norm2_g, peer_w_query, peer_sub_keys_1,
           peer_sub_keys_2, peer_expert_u, peer_expert_v):
    b, s, d = x.shape
    x2 = x.reshape(b * s, d)
    for l in range(norm1_g.shape[0]):
        x2 = _layer(x2, b, s, norm1_g[l], w_in[l], forget_bias[l], q_norm_g[l], k_norm_g[l],
                    pool_group_w[l], pool_scale[l], w_branch_attn[l], w_branch_pool[l], w_out[l],
                    norm2_g[l], peer_w_query[l], peer_sub_keys_1[l], peer_sub_keys_2[l],
                    peer_expert_u[l], peer_expert_v[l])
    return x2.reshape(b, s, d)
```

```python
import functools

import jax
import jax.numpy as jnp
from jax import lax
from jax.experimental import pallas as pl
from jax.experimental.pallas import tpu as pltpu

F32 = jnp.float32
BF16 = jnp.bfloat16

RMS_EPS = 1e-6
FOX_HEADS = 8
HEAD_DIM = 128
FOX_WIDTH = FOX_HEADS * HEAD_DIM
POOL_WINDOWS = (2, 4, 8, 16)
POOL_WIDTH = 1024
POOL_GROUP_DIM = POOL_WIDTH // len(POOL_WINDOWS)
POOL_HALO = 16
PEER_HEADS = 8
PEER_KEYS = 128
PEER_HALF = 128
PEER_TOPK = 16
PEER_PAIRS = PEER_HEADS * PEER_TOPK

LANES = 128
VMEM_LIMIT_BYTES = 56 * 1024 * 1024


def _params(*sem):
    return pltpu.CompilerParams(dimension_semantics=sem, vmem_limit_bytes=VMEM_LIMIT_BYTES)


def _rms(x, g):
    ms = jnp.mean(x * x, axis=-1, keepdims=True)
    return x * lax.rsqrt(ms + RMS_EPS) * g


def _norm_proj_kernel(x_ref, g_ref, w_ref, gain_ref, o_ref, h_ref, *, mode, n_first, tn):
    j = pl.program_id(1)

    @pl.when(j == 0)
    def _():
        h_ref[...] = _rms(x_ref[...], g_ref[...]).astype(BF16)

    y = jnp.dot(h_ref[...], w_ref[...], preferred_element_type=F32)

    if mode == "qk":
        for hh in range(tn // HEAD_DIM):
            sl = slice(hh * HEAD_DIM, (hh + 1) * HEAD_DIM)
            blk = y[:, sl]
            ms = jnp.mean(blk * blk, axis=-1, keepdims=True)
            o_ref[:, sl] = (blk * lax.rsqrt(ms + RMS_EPS) * gain_ref[:, sl]).astype(o_ref.dtype)
    elif mode == "vt":
        o_ref[...] = y.T.astype(o_ref.dtype)
    else:
        gate = 1.0 / (1.0 + jnp.exp(-y))
        o_ref[...] = jnp.where(j >= n_first, gate, y).astype(o_ref.dtype)


def _norm_proj(x2, g, w, gain, *, mode, n_first, out_dtype, tm, tn):
    t, d = x2.shape
    n = w.shape[1]
    kern = functools.partial(_norm_proj_kernel, mode=mode, n_first=n_first, tn=tn)
    if mode == "vt":
        out_spec = pl.BlockSpec((tn, tm), lambda i, j: (j, i))
        out_shape = jax.ShapeDtypeStruct((n, t), out_dtype)
    else:
        out_spec = pl.BlockSpec((tm, tn), lambda i, j: (i, j))
        out_shape = jax.ShapeDtypeStruct((t, n), out_dtype)
    return pl.pallas_call(
        kern,
        grid=(t // tm, n // tn),
        in_specs=[
            pl.BlockSpec((tm, d), lambda i, j: (i, 0)),
            pl.BlockSpec((1, d), lambda i, j: (0, 0)),
            pl.BlockSpec((d, tn), lambda i, j: (0, j)),
            pl.BlockSpec((1, tn), lambda i, j: (0, j)),
        ],
        out_specs=out_spec,
        out_shape=out_shape,
        scratch_shapes=[pltpu.VMEM((tm, d), BF16)],
        compiler_params=_params("parallel", "arbitrary"),
        name="norm_proj_" + mode,
    )(x2, g, w, gain)


def _logf_kernel(x_ref, g_ref, w_ref, b_ref, ct_ref, carry_ref, *, tm):
    s = pl.program_id(1)

    @pl.when(s == 0)
    def _():
        carry_ref[...] = jnp.zeros_like(carry_ref)

    h = _rms(x_ref[...], g_ref[...]).astype(BF16)
    fl = jnp.dot(h, w_ref[...], preferred_element_type=F32) + b_ref[...]
    lf = jnp.minimum(fl, 0.0) - jnp.log(1.0 + jnp.exp(-jnp.abs(fl)))
    r = lax.broadcasted_iota(jnp.int32, (tm, tm), 0)
    c = lax.broadcasted_iota(jnp.int32, (tm, tm), 1)
    tri = (r >= c).astype(BF16)
    p1 = lf.astype(BF16)
    r1 = lf - p1.astype(F32)
    p2 = r1.astype(BF16)
    p3 = (r1 - p2.astype(F32)).astype(BF16)
    cs = (jnp.dot(tri, p1, preferred_element_type=F32)
          + jnp.dot(tri, p2, preferred_element_type=F32)
          + jnp.dot(tri, p3, preferred_element_type=F32)) + carry_ref[...]
    carry_ref[...] = cs[tm - 1:tm, :]
    c2 = cs
    hr = lax.broadcasted_iota(jnp.int32, (LANES, FOX_WIDTH), 0)
    hc = lax.broadcasted_iota(jnp.int32, (LANES, FOX_WIDTH), 1)
    spread = (hc // HEAD_DIM == hr).astype(BF16)
    q1 = c2.astype(BF16)
    s1 = c2 - q1.astype(F32)
    q2 = s1.astype(BF16)
    q3 = (s1 - q2.astype(F32)).astype(BF16)
    ct_ref[...] = (jnp.dot(q1, spread, preferred_element_type=F32)
                   + jnp.dot(q2, spread, preferred_element_type=F32)
                   + jnp.dot(q3, spread, preferred_element_type=F32))


def _logf_cumsum(x2, g, w_fl, b_fl, *, batch, seq, tm):
    t, d = x2.shape
    ns = seq // tm
    return pl.pallas_call(
        functools.partial(_logf_kernel, tm=tm),
        grid=(batch, ns),
        in_specs=[
            pl.BlockSpec((tm, d), lambda b, s: (b * ns + s, 0)),
            pl.BlockSpec((1, d), lambda b, s: (0, 0)),
            pl.BlockSpec((d, LANES), lambda b, s: (0, 0)),
            pl.BlockSpec((1, LANES), lambda b, s: (0, 0)),
        ],
        out_specs=pl.BlockSpec((tm, FOX_WIDTH), lambda b, s: (b * ns + s, 0)),
        out_shape=jax.ShapeDtypeStruct((t, FOX_WIDTH), F32),
        scratch_shapes=[pltpu.VMEM((1, LANES), F32)],
        compiler_params=_params("parallel", "arbitrary"),
        name="logf_cumsum",
    )(x2, g, w_fl, b_fl)


ATTN_HEADS_PER_STEP = 2


def _attn_kernel(q_ref, k_ref, vt_ref, ck_ref, o_ref, m_ref, l_ref, acc_ref, *, tq, tk, nk):
    qi = pl.program_id(2)
    kj = pl.program_id(3)

    @pl.when(kj == 0)
    def _():
        m_ref[...] = jnp.full_like(m_ref, -jnp.inf)
        l_ref[...] = jnp.zeros_like(l_ref)
        acc_ref[...] = jnp.zeros_like(acc_ref)

    def update(on_diagonal):
        for hh in range(ATTN_HEADS_PER_STEP):
            sl = slice(hh * HEAD_DIM, (hh + 1) * HEAD_DIM)
            s = lax.dot_general(k_ref[:, sl], q_ref[:, sl], (((1,), (1,)), ((), ())),
                                preferred_element_type=F32)
            s = s - jnp.concatenate([ck_ref[:, sl]] * (tq // LANES), axis=1)
            if on_diagonal:
                keys = kj * tk + lax.broadcasted_iota(jnp.int32, (tk, tq), 0)
                queries = qi * tq + lax.broadcasted_iota(jnp.int32, (tk, tq), 1)
                s = jnp.where(queries >= keys, s, -jnp.inf)
            m_prev = m_ref[hh]
            m_new = jnp.maximum(m_prev, jnp.max(s, axis=0, keepdims=True))
            alpha = jnp.exp(m_prev - m_new)
            p = jnp.exp(s - m_new)
            l_ref[hh] = alpha * l_ref[hh] + jnp.sum(p, axis=0, keepdims=True)
            acc_ref[hh] = alpha * acc_ref[hh] + jnp.dot(vt_ref[sl, :], p.astype(BF16),
                                                        preferred_element_type=F32)
            m_ref[hh] = m_new

    first_key = kj * tk
    last_query = qi * tq + (tq - 1)

    @pl.when(first_key + (tk - 1) <= qi * tq)
    def _():
        update(False)

    @pl.when((first_key + (tk - 1) > qi * tq) & (first_key <= last_query))
    def _():
        update(True)

    @pl.when(kj == nk - 1)
    def _():
        for hh in range(ATTN_HEADS_PER_STEP):
            sl = slice(hh * HEAD_DIM, (hh + 1) * HEAD_DIM)
            o_ref[:, sl] = (acc_ref[hh] / l_ref[hh]).T.astype(o_ref.dtype)


def _attention(qk, vt, ck, *, batch, seq, tq, tk):
    t = qk.shape[0]
    nq, nk = seq // tq, seq // tk
    hp = ATTN_HEADS_PER_STEP
    width = hp * HEAD_DIM
    n_hb = FOX_HEADS // hp

    def kv_block(b, qi, kj):
        return b * nk + jnp.minimum(kj, (qi * tq + (tq - 1)) // tk)

    return pl.pallas_call(
        functools.partial(_attn_kernel, tq=tq, tk=tk, nk=nk),
        grid=(batch, n_hb, nq, nk),
        in_specs=[
            pl.BlockSpec((tq, width), lambda b, h, qi, kj: (b * nq + qi, h)),
            pl.BlockSpec((tk, width), lambda b, h, qi, kj: (kv_block(b, qi, kj), n_hb + h)),
            pl.BlockSpec((width, tk), lambda b, h, qi, kj: (h, kv_block(b, qi, kj))),
            pl.BlockSpec((tk, width), lambda b, h, qi, kj: (kv_block(b, qi, kj), h)),
        ],
        out_specs=pl.BlockSpec((tq, width), lambda b, h, qi, kj: (b * nq + qi, h)),
        out_shape=jax.ShapeDtypeStruct((t, FOX_WIDTH), BF16),
        scratch_shapes=[pltpu.VMEM((hp, 1, tq), F32), pltpu.VMEM((hp, 1, tq), F32),
                        pltpu.VMEM((hp, HEAD_DIM, tq), F32)],
        compiler_params=_params("parallel", "parallel", "parallel", "arbitrary"),
        name="fox_attention",
    )(qk, qk, vt, ck)


def _pool_kernel(p_ref, w_ref, sc_ref, o_ref, ext_ref, *, tm):
    s = pl.program_id(1)

    @pl.when(s == 0)
    def _():
        ext_ref[0:POOL_HALO, :] = jnp.zeros((POOL_HALO, POOL_WIDTH), F32)

    @pl.when(s > 0)
    def _():
        ext_ref[0:POOL_HALO, :] = ext_ref[tm:tm + POOL_HALO, :]

    ext_ref[POOL_HALO:POOL_HALO + tm, :] = p_ref[...]
    pos = s * tm + lax.broadcasted_iota(jnp.int32, (tm, 1), 0)
    for gi, w in enumerate(POOL_WINDOWS):
        cols = slice(gi * POOL_GROUP_DIM, (gi + 1) * POOL_GROUP_DIM)
        cur = ext_ref[POOL_HALO:POOL_HALO + tm, cols]
        acc = cur
        for lag in range(1, w):
            acc = acc + ext_ref[POOL_HALO - lag:POOL_HALO - lag + tm, cols]
        count = jnp.minimum(pos + 1, w).astype(F32)
        mixed = acc / count - cur
        y = jnp.dot(mixed.astype(BF16), w_ref[gi], preferred_element_type=F32)
        o_ref[:, cols] = (y * sc_ref[:, cols]).astype(o_ref.dtype)


def _pool(pg, w_groups, scale, *, batch, seq, tm):
    t = pg.shape[0]
    ns = seq // tm
    return pl.pallas_call(
        functools.partial(_pool_kernel, tm=tm),
        grid=(batch, ns),
        in_specs=[
            pl.BlockSpec((tm, POOL_WIDTH), lambda b, s: (b * ns + s, 0)),
            pl.BlockSpec(w_groups.shape, lambda b, s: (0, 0, 0)),
            pl.BlockSpec((1, POOL_WIDTH), lambda b, s: (0, 0)),
        ],
        out_specs=pl.BlockSpec((tm, POOL_WIDTH), lambda b, s: (b * ns + s, 0)),
        out_shape=jax.ShapeDtypeStruct((t, POOL_WIDTH), BF16),
        scratch_shapes=[pltpu.VMEM((tm + POOL_HALO, POOL_WIDTH), F32)],
        compiler_params=_params("parallel", "arbitrary"),
        name="multiscale_pool",
    )(pg, w_groups, scale)


def _merge_kernel(a_ref, p_ref, wa_ref, wp_ref, ga_ref, gp_ref, o_ref):
    ya = jnp.dot(a_ref[...], wa_ref[...], preferred_element_type=F32)
    yp = jnp.dot(p_ref[...], wp_ref[...], preferred_element_type=F32)
    o_ref[...] = (ga_ref[...] * ya + gp_ref[...] * yp).astype(o_ref.dtype)


def _merge(attn, pooled, wa, wp, pg, *, d_model, tm, tn):
    t = attn.shape[0]
    ga_off = POOL_WIDTH // tn
    gp_off = (POOL_WIDTH + d_model) // tn
    return pl.pallas_call(
        _merge_kernel,
        grid=(t // tm, d_model // tn),
        in_specs=[
            pl.BlockSpec((tm, FOX_WIDTH), lambda i, j: (i, 0)),
            pl.BlockSpec((tm, POOL_WIDTH), lambda i, j: (i, 0)),
            pl.BlockSpec((FOX_WIDTH, tn), lambda i, j: (0, j)),
            pl.BlockSpec((POOL_WIDTH, tn), lambda i, j: (0, j)),
            pl.BlockSpec((tm, tn), lambda i, j: (i, ga_off + j)),
            pl.BlockSpec((tm, tn), lambda i, j: (i, gp_off + j)),
        ],
        out_specs=pl.BlockSpec((tm, tn), lambda i, j: (i, j)),
        out_shape=jax.ShapeDtypeStruct((t, d_model), BF16),
        compiler_params=_params("parallel", "arbitrary"),
        name="gated_merge",
    )(attn, pooled, wa, wp, pg, pg)


def _out_proj_kernel(m_ref, w_ref, x_ref, o_ref):
    o_ref[...] = x_ref[...] + jnp.dot(m_ref[...], w_ref[...], preferred_element_type=F32)


def _out_proj(merged, wo, x2, *, tm, tn):
    t, d = x2.shape
    return pl.pallas_call(
        _out_proj_kernel,
        grid=(t // tm, d // tn),
        in_specs=[
            pl.BlockSpec((tm, d), lambda i, j: (i, 0)),
            pl.BlockSpec((d, tn), lambda i, j: (0, j)),
            pl.BlockSpec((tm, tn), lambda i, j: (i, j)),
        ],
        out_specs=pl.BlockSpec((tm, tn), lambda i, j: (i, j)),
        out_shape=jax.ShapeDtypeStruct((t, d), F32),
        compiler_params=_params("parallel", "arbitrary"),
        name="out_proj_residual",
    )(merged, wo, x2)


def _extract_max(s, order):
    m = jnp.max(s, axis=0, keepdims=True)
    am = jnp.min(jnp.where(s == m, order, jnp.iinfo(jnp.int32).max), axis=0, keepdims=True)
    return m, am, order == am


ROUTE_HEADS_PER_STEP = 2


def _topk_tiles(tiles, val_ref, idx_ref):
    rows = lax.broadcasted_iota(jnp.int32, tiles[0].shape, 0)

    def step(i, carry):
        out = []
        for n, s in enumerate(carry):
            m, am, hit = _extract_max(s, rows)
            val_ref[n, pl.ds(i, 1), :] = m
            idx_ref[n, pl.ds(i, 1), :] = am
            out.append(jnp.where(hit, -jnp.inf, s))
        return tuple(out)

    lax.fori_loop(0, PEER_TOPK, step, tuple(tiles))


def _candidates(v1, i1, v2, i2, tm):
    vals, poss, idxs = [], [], []
    for a in range(8):
        nb = PEER_TOPK // (a + 1)
        vals.append(v1[a:a + 1, :] + v2[0:nb, :])
        poss.append(lax.broadcasted_iota(jnp.int32, (nb, tm), 0) + a * PEER_TOPK)
        idxs.append(i1[a:a + 1, :] * PEER_KEYS + i2[0:nb, :])
    vals.append(v1[8:16, :] + v2[0:1, :])
    poss.append((lax.broadcasted_iota(jnp.int32, (8, tm), 0) + 8) * PEER_TOPK)
    idxs.append(i1[8:16, :] * PEER_KEYS + i2[0:1, :])
    return (jnp.concatenate(vals, axis=0), jnp.concatenate(poss, axis=0),
            jnp.concatenate(idxs, axis=0))


def _topk_candidates(cands, pos, cidxs, val_ref, idx_ref):
    def step(i, carry):
        out = []
        for n, s in enumerate(carry):
            m, _, hit = _extract_max(s, pos)
            val_ref[n, pl.ds(i, 1), :] = m
            idx_ref[n, pl.ds(i, 1), :] = jnp.max(jnp.where(hit, cidxs[n], -1), axis=0,
                                                 keepdims=True)
            out.append(jnp.where(hit, -jnp.inf, s))
        return tuple(out)

    lax.fori_loop(0, PEER_TOPK, step, tuple(cands))


def _route_kernel(x_ref, g_ref, wq_ref, k1_ref, k2_ref, idx_ref, gate2_ref,
                  q_ref, v_ref, i_ref, ts_ref, te_ref, gate_ref, *, tm):
    xn = _rms(x_ref[...], g_ref[...]).astype(BF16)
    q_ref[...] = jnp.dot(xn, wq_ref[...], preferred_element_type=F32).astype(BF16)
    nh = ROUTE_HEADS_PER_STEP

    def heads(hg, carry):
        tiles = []
        for n in range(nh):
            h = hg * nh + n
            off = pl.multiple_of(h * (2 * PEER_HALF), 2 * PEER_HALF)
            q1 = q_ref[:, pl.ds(off, PEER_HALF)]
            q2 = q_ref[:, pl.ds(off + PEER_HALF, PEER_HALF)]
            dn = (((1,), (1,)), ((), ()))
            tiles.append(lax.dot_general(k1_ref[h], q1, dn, preferred_element_type=F32))
            tiles.append(lax.dot_general(k2_ref[h], q2, dn, preferred_element_type=F32))
        _topk_tiles(tiles, v_ref, i_ref)
        cands, cidxs = [], []
        for n in range(nh):
            cand, pos, cidx = _candidates(v_ref.at[2 * n], i_ref.at[2 * n],
                                          v_ref.at[2 * n + 1], i_ref.at[2 * n + 1], tm)
            cands.append(cand)
            cidxs.append(cidx)
        _topk_candidates(cands, pos, cidxs, ts_ref, te_ref)
        for n in range(nh):
            ts = ts_ref[n]
            e = jnp.exp(ts - jnp.max(ts, axis=0, keepdims=True))
            row0 = pl.multiple_of((hg * nh + n) * PEER_TOPK, PEER_TOPK)
            gate_ref[pl.ds(row0, PEER_TOPK), :] = e / jnp.sum(e, axis=0, keepdims=True)
            idx_ref[pl.ds(row0, PEER_TOPK), :] = te_ref[n]
        return carry

    lax.fori_loop(0, PEER_HEADS // nh, heads, 0)

    r = lax.broadcasted_iota(jnp.int32, (2 * PEER_PAIRS, PEER_PAIRS), 0)
    c = lax.broadcasted_iota(jnp.int32, (2 * PEER_PAIRS, PEER_PAIRS), 1)
    dup = ((r >> 1) == c).astype(BF16)
    gate = gate_ref[...]
    p1 = gate.astype(BF16)
    r1 = gate - p1.astype(F32)
    p2 = r1.astype(BF16)
    p3 = (r1 - p2.astype(F32)).astype(BF16)
    gate2_ref[...] = (jnp.dot(dup, p1, preferred_element_type=F32)
                      + jnp.dot(dup, p2, preferred_element_type=F32)
                      + jnp.dot(dup, p3, preferred_element_type=F32))


def _route(x1, g, wq, k1, k2, *, tm):
    t, d = x1.shape
    nq = wq.shape[1]
    nh = ROUTE_HEADS_PER_STEP
    return pl.pallas_call(
        functools.partial(_route_kernel, tm=tm),
        grid=(t // tm,),
        in_specs=[
            pl.BlockSpec((tm, d), lambda i: (i, 0)),
            pl.BlockSpec((1, d), lambda i: (0, 0)),
            pl.BlockSpec((d, nq), lambda i: (0, 0)),
            pl.BlockSpec(k1.shape, lambda i: (0, 0, 0)),
            pl.BlockSpec(k2.shape, lambda i: (0, 0, 0)),
        ],
        out_specs=[
            pl.BlockSpec((PEER_PAIRS, tm), lambda i: (0, i)),
            pl.BlockSpec((2 * PEER_PAIRS, tm), lambda i: (0, i)),
        ],
        out_shape=[jax.ShapeDtypeStruct((PEER_PAIRS, t), jnp.int32),
                   jax.ShapeDtypeStruct((2 * PEER_PAIRS, t), F32)],
        scratch_shapes=[
            pltpu.VMEM((tm, nq), BF16),
            pltpu.VMEM((2 * nh, PEER_TOPK, tm), F32), pltpu.VMEM((2 * nh, PEER_TOPK, tm), jnp.int32),
            pltpu.VMEM((nh, PEER_TOPK, tm), F32), pltpu.VMEM((nh, PEER_TOPK, tm), jnp.int32),
            pltpu.VMEM((PEER_PAIRS, tm), F32),
        ],
        compiler_params=_params("parallel"),
        name="peer_route",
    )(x1, g, wq, k1, k2)


def _pack_kernel(u_ref, v_ref, o_ref):
    te, d = u_ref.shape
    half = d // 2
    lo = jnp.concatenate([u_ref[:, :half], v_ref[:, :half]], axis=1).astype(BF16)
    hi = jnp.concatenate([u_ref[:, half:], v_ref[:, half:]], axis=1).astype(BF16)
    r = lax.broadcasted_iota(jnp.int32, (2 * te, te), 0)
    c = lax.broadcasted_iota(jnp.int32, (2 * te, te), 1)
    rows = (jnp.dot((r == 2 * c).astype(BF16), lo, preferred_element_type=F32)
            + jnp.dot((r == 2 * c + 1).astype(BF16), hi, preferred_element_type=F32))
    o_ref[:, 0, :] = pltpu.bitcast(rows.astype(BF16), jnp.int32)


def _pack_tables(u, v, *, te):
    e, d = u.shape
    return pl.pallas_call(
        _pack_kernel,
        grid=(e // te,),
        in_specs=[pl.BlockSpec((te, d), lambda i: (i, 0)),
                  pl.BlockSpec((te, d), lambda i: (i, 0))],
        out_specs=pl.BlockSpec((te, 1, d), lambda i: (i, 0, 0)),
        out_shape=jax.ShapeDtypeStruct((e, 1, d), jnp.int32),
        compiler_params=_params("parallel"),
        name="peer_pack_tables",
    )(u, v)


SUBLANES = 8
PEER_ROW_GROUPS = PEER_PAIRS // SUBLANES
PEER_NBUF = 8
PEER_AHEAD = 6


def _gelu_tanh(a):
    c = 0.7978845608028654
    return 0.5 * a * (1.0 + jnp.tanh(c * (a + 0.044715 * (a * a * a))))


def _peer_kernel(idx_ref, x_ref, g_ref, gate_ref, tab_ref, o_ref, xn_ref, *rest, tb):
    bufs, sem = rest[:PEER_NBUF], rest[PEER_NBUF]
    d = x_ref.shape[1]
    half = d // 2
    rows = 2 * SUBLANES
    xn_ref[...] = _rms(x_ref[...], g_ref[...])
    lane = lax.broadcasted_iota(jnp.int32, (PEER_ROW_GROUPS, rows, tb), 2)
    even2 = (lax.broadcasted_iota(jnp.int32, (rows, 1), 0) & 1) == 0
    even3 = (lax.broadcasted_iota(jnp.int32, (1, rows, LANES), 1) & 1) == 0

    def row_copy(e, slot, k):
        return pltpu.make_async_copy(
            tab_ref.at[e], bufs[slot].at[k // SUBLANES, pl.ds(k % SUBLANES, 1), :], sem.at[slot])

    def issue(tok, slot):
        base = tok * PEER_PAIRS
        for k in range(PEER_PAIRS):
            row_copy(idx_ref[0, 0, base + k], slot, k).start(priority=k % 2)

    def wait(slot):
        pltpu.make_async_copy(bufs[slot], bufs[slot], sem.at[slot]).wait()

    def compute(tok, slot):
        wf = pltpu.bitcast(bufs[slot][...], BF16).astype(F32)
        xalt = jnp.where(even2, xn_ref[pl.ds(tok, 1), 0:half], xn_ref[pl.ds(tok, 1), half:d])
        prod = wf[:, :, :half] * xalt[None]
        part = prod[:, :, 0:LANES]
        for c in range(1, half // LANES):
            part = part + prod[:, :, c * LANES:(c + 1) * LANES]
        pair = part + pltpu.roll(part, rows - 1, axis=1)
        pair = jnp.where(even3, pair, pltpu.roll(pair, 1, axis=1))
        a = jnp.sum(pair, axis=-1, keepdims=True)
        gate = jnp.sum(jnp.where(lane == tok, gate_ref[...], 0.0), axis=-1, keepdims=True)
        hid = _gelu_tanh(a) * gate
        s = jnp.sum(hid * wf[:, :, half:], axis=0)
        s8 = s[0:SUBLANES] + s[SUBLANES:rows]
        s8 = s8 + pltpu.roll(s8, 2, axis=0)
        s8 = s8 + pltpu.roll(s8, 4, axis=0)
        o_ref[pl.ds(tok, 1), 0:half] = x_ref[pl.ds(tok, 1), 0:half] + s8[0:1]
        o_ref[pl.ds(tok, 1), half:d] = x_ref[pl.ds(tok, 1), half:d] + s8[1:2]

    def step(tok, j, prefetch):
        wait(j)
        if prefetch:
            issue(tok + PEER_AHEAD, (j + PEER_AHEAD) % PEER_NBUF)
        compute(tok, j)

    for j in range(PEER_AHEAD):
        issue(j, j)

    def group(p, carry):
        for j in range(PEER_NBUF):
            step(p * PEER_NBUF + j, j, True)
        return carry

    n_groups = tb // PEER_NBUF
    lax.fori_loop(0, n_groups - 1, group, 0)
    for j in range(PEER_NBUF):
        step((n_groups - 1) * PEER_NBUF + j, j, j + PEER_AHEAD < PEER_NBUF)


def _peer(idx_blocks, x1, g, gate3, table, *, tb):
    t, d = x1.shape
    buf = pltpu.VMEM((PEER_ROW_GROUPS, SUBLANES, d), jnp.int32)
    return pl.pallas_call(
        functools.partial(_peer_kernel, tb=tb),
        grid=(t // tb,),
        in_specs=[
            pl.BlockSpec((1, 1, tb * PEER_PAIRS), lambda i: (i, 0, 0), memory_space=pltpu.SMEM),
            pl.BlockSpec((tb, d), lambda i: (i, 0)),
            pl.BlockSpec((1, d), lambda i: (0, 0)),
            pl.BlockSpec((PEER_ROW_GROUPS, 2 * SUBLANES, tb), lambda i: (0, 0, i)),
            pl.BlockSpec(memory_space=pl.ANY),
        ],
        out_specs=pl.BlockSpec((tb, d), lambda i: (i, 0)),
        out_shape=jax.ShapeDtypeStruct((t, d), F32),
        scratch_shapes=[pltpu.VMEM((tb, d), F32)] + [buf] * PEER_NBUF
                       + [pltpu.SemaphoreType.DMA((PEER_NBUF,))],
        compiler_params=_params("arbitrary"),
        name="peer_experts",
    )(idx_blocks, x1, g, gate3, table)


def _layer(x2, batch, seq, norm1_g, w_in, forget_bias, q_norm_g, k_norm_g, pool_group_w,
           pool_scale, w_branch_attn, w_branch_pool, w_out, norm2_g, peer_w_query,
           peer_sub_keys_1, peer_sub_keys_2, peer_expert_u, peer_expert_v):
    t, d = x2.shape
    qkv_cols = 3 * FOX_WIDTH
    fl_end = qkv_cols + FOX_HEADS
    tm = min(1024, t)
    tn = 512

    g1 = norm1_g.reshape(1, d)
    w_qk = w_in[:, :2 * FOX_WIDTH].astype(BF16)
    w_v = w_in[:, 2 * FOX_WIDTH:qkv_cols].astype(BF16)
    w_fl = jnp.pad(w_in[:, qkv_cols:fl_end], ((0, 0), (0, LANES - FOX_HEADS))).astype(BF16)
    b_fl = jnp.pad(forget_bias.astype(F32), (0, LANES - FOX_HEADS)).reshape(1, LANES)
    w_pg = w_in[:, fl_end:].astype(BF16)

    gain = jnp.concatenate([
        jnp.tile(q_norm_g.astype(F32) * (HEAD_DIM ** -0.5), FOX_HEADS),
        jnp.tile(k_norm_g.astype(F32), FOX_HEADS)]).reshape(1, 2 * FOX_WIDTH)
    qk = _norm_proj(x2, g1, w_qk, gain, mode="qk", n_first=0, out_dtype=BF16, tm=tm, tn=tn)
    vt = _norm_proj(x2, g1, w_v, jnp.ones((1, FOX_WIDTH), F32), mode="vt", n_first=0,
                    out_dtype=BF16, tm=tm, tn=tn)
    pg = _norm_proj(x2, g1, w_pg, jnp.ones((1, w_pg.shape[1]), F32), mode="pg",
                    n_first=POOL_WIDTH // tn, out_dtype=F32, tm=tm, tn=tn)
    ck = _logf_cumsum(x2, g1, w_fl, b_fl, batch=batch, seq=seq, tm=min(512, seq))
    ta = min(512, seq)
    attn = _attention(qk, vt, ck, batch=batch, seq=seq, tq=ta, tk=ta)
    pooled = _pool(pg, pool_group_w.astype(BF16), pool_scale.reshape(1, POOL_WIDTH).astype(F32),
                   batch=batch, seq=seq, tm=min(512, seq))
    merged = _merge(attn, pooled, w_branch_attn.astype(BF16), w_branch_pool.astype(BF16), pg,
                    d_model=d, tm=tm, tn=tn)
    x1 = _out_proj(merged, w_out.astype(BF16), x2, tm=tm, tn=tn)

    idx_t, gate2 = _route(x1, norm2_g.reshape(1, d), peer_w_query.astype(BF16),
                          peer_sub_keys_1.astype(BF16), peer_sub_keys_2.astype(BF16), tm=LANES)
    tb = LANES
    idx_blocks = idx_t.T.reshape(t // tb, 1, tb * PEER_PAIRS)
    gate3 = gate2.reshape(PEER_ROW_GROUPS, 2 * SUBLANES, t)
    table = _pack_tables(peer_expert_u, peer_expert_v, te=256)
    return _peer(idx_blocks, x1, norm2_g.reshape(1, d), gate3, table, tb=tb)


def kernel(x, norm1_g, w_in, forget_bias, q_norm_g, k_norm_g, pool_group_w, pool_scale,
           w_branch_attn, w_branch_pool, w_out, norm2_g, peer_w_query, peer_sub_keys_1,
           peer_sub_keys_2, peer_expert_u, peer_expert_v):
    b, s, d = x.shape
    x2 = x.reshape(b * s, d)
    for l in range(norm1_g.shape[0]):
        x2 = _layer(x2, b, s, norm1_g[l], w_in[l], forget_bias[l], q_norm_g[l], k_norm_g[l],
                    pool_group_w[l], pool_scale[l], w_branch_attn[l], w_branch_pool[l], w_out[l],
                    norm2_g[l], peer_w_query[l], peer_sub_keys_1[l], peer_sub_keys_2[l],
                    peer_expert_u[l], peer_expert_v[l])
    return x2.reshape(b, s, d)
```
